```python
import math
import jax, jax.numpy as jnp
from jax import lax
import numpy as np

D_MODEL = 1024
BATCH = 32
SEQ = 2048
DEPTH = 2

ROPE_THETA = 500000.0
ROPE_FRACTION = 4
NORM_EPS = 1e-6
Q_BLOCK = 128

H_A = 4
DH_A = 64
H_B = 4
DK_B = 128
DV_B = 128
CONV_B = 4
CHUNK_B = 64
H_C = 8
Q_LORA_C = 256
KV_LORA_C = 128
NOPE_C = 64
ROPE_C = 32
V_C = 64
H_D = 8
DH_D = 64
DIL_GROUPS = ((128, 1), (512, 4), (2048, 16))
N_DIL = 3
D_BLOCK = 128

N_BRANCH = 4
BRANCH_W = 512
DEEPNORM_ALPHA = (2.0 * DEPTH) ** 0.25
DEEPNORM_BETA = (8.0 * DEPTH) ** -0.25

IN_SPLITS = (
    H_A * 2 * DH_A, H_A * 2 * DH_A, H_A * 2 * DH_A,
    H_B * DK_B, H_B * DK_B, H_B * DV_B, H_B, H_B,
    Q_LORA_C, KV_LORA_C, ROPE_C,
    3 * N_DIL * H_D * DH_D,
    N_BRANCH * BRANCH_W,
    N_BRANCH * D_MODEL,
)
D_IN = sum(IN_SPLITS)

kernel_name = 'hybrid_gated_diff_delta_mla_dilated'


def _layernorm(x):
    xf = x.astype(jnp.float32)
    mu = jnp.mean(xf, -1, keepdims=True)
    var = jnp.mean(jnp.square(xf - mu), -1, keepdims=True)
    return ((xf - mu) * lax.rsqrt(var + NORM_EPS)).astype(x.dtype)


def _rmsnorm(x, g):
    xf = x.astype(jnp.float32)
    y = xf * lax.rsqrt(jnp.mean(jnp.square(xf), -1, keepdims=True) + NORM_EPS)
    return (y * g.astype(jnp.float32)).astype(x.dtype)


def _l2norm(x):
    xf = x.astype(jnp.float32)
    return (xf * lax.rsqrt(jnp.sum(jnp.square(xf), -1, keepdims=True) + NORM_EPS)).astype(x.dtype)


def _rope(x, pos, rot_dim):
    half = rot_dim // 2
    inv_freq = ROPE_THETA ** (-jnp.arange(half, dtype=jnp.float32) / half)
    ang = pos.astype(jnp.float32)[..., None] * inv_freq
    cos = jnp.cos(ang)[:, :, None, :]
    sin = jnp.sin(ang)[:, :, None, :]
    xr = x[..., :rot_dim].astype(jnp.float32)
    x1, x2 = xr[..., :half], xr[..., half:]
    rot = jnp.concatenate([x1 * cos - x2 * sin, x2 * cos + x1 * sin], -1).astype(x.dtype)
    return jnp.concatenate([rot, x[..., rot_dim:]], -1)


def _causal_block_sweep(attend, seq):
    return jnp.concatenate([attend(s, s + Q_BLOCK) for s in range(0, seq, Q_BLOCK)], axis=1)


def _causal_mask(s, e):
    return jnp.arange(s, e)[:, None] >= jnp.arange(e)[None, :]


def _diff_attention(q, k, v, lam, pos):
    B, S = q.shape[:2]
    rot = DH_A // ROPE_FRACTION
    q = _rope(q.reshape(B, S, H_A * 2, DH_A), pos, rot).reshape(B, S, H_A, 2, DH_A)
    k = _rope(k.reshape(B, S, H_A * 2, DH_A), pos, rot).reshape(B, S, H_A, 2, DH_A)
    scale = DH_A ** -0.5

    def attend(s, e):
        sc = jnp.einsum('bqhmd,bkhmd->bhmqk', q[:, s:e], k[:, :e]).astype(jnp.float32) * scale
        sc = jnp.where(_causal_mask(s, e), sc, -jnp.inf)
        p = jax.nn.softmax(sc, axis=-1)
        p = p[:, :, 0] - lam * p[:, :, 1]
        return jnp.einsum('bhqk,bkhd->bqhd', p.astype(v.dtype), v[:, :e])

    return _causal_block_sweep(attend, S)


def _short_conv(x, w):
    S = x.shape[1]
    xp = jnp.pad(x, ((0, 0), (CONV_B - 1, 0), (0, 0)))
    return sum(xp[:, j:j + S] * w[j] for j in range(CONV_B))


def _gated_delta_rule(q, k, v, g, beta):
    B, S, H, dk = q.shape
    dv = v.shape[-1]
    C = CHUNK_B
    N = S // C
    f32 = jnp.float32

    def chunks(t):
        return t.astype(f32).reshape(B, N, C, H, -1).transpose(0, 3, 1, 2, 4)

    qc = chunks(q) * (dk ** -0.5)
    kc = chunks(k)
    vc = chunks(v)
    bc = beta.astype(f32).reshape(B, N, C, H).transpose(0, 3, 1, 2)
    gc = jnp.cumsum(g.astype(f32).reshape(B, N, C, H).transpose(0, 3, 1, 2), axis=-1)
    tril = jnp.tril(jnp.ones((C, C), bool))
    strict = jnp.tril(jnp.ones((C, C), bool), -1)
    diff = gc[..., :, None] - gc[..., None, :]
    decay = jnp.where(tril, jnp.exp(jnp.where(tril, diff, 0.0)), 0.0)
    k_beta = kc * bc[..., None]
    lower = jnp.where(strict, jnp.einsum('bhncd,bhnmd->bhncm', k_beta, kc) * decay, 0.0)
    a_mat = jnp.eye(C, dtype=f32) + lower
    rhs = jnp.concatenate([vc * bc[..., None], k_beta * jnp.exp(gc)[..., None]], -1)
    sol = lax.linalg.triangular_solve(a_mat, rhs, left_side=True, lower=True, unit_diagonal=True)
    u, w = sol[..., :dv], sol[..., dv:]
    qk = jnp.where(tril, jnp.einsum('bhncd,bhnmd->bhncm', qc, kc) * decay, 0.0)

    def step(state, xs):
        q_i, k_i, u_i, w_i, qk_i, g_i = xs
        v_new = u_i - jnp.einsum('bhcd,bhde->bhce', w_i, state)
        o = (jnp.einsum('bhcd,bhde->bhce', q_i * jnp.exp(g_i)[..., None], state)
             + jnp.einsum('bhcm,bhme->bhce', qk_i, v_new))
        g_last = g_i[..., -1]
        state = (state * jnp.exp(g_last)[..., None, None]
                 + jnp.einsum('bhcd,bhce->bhde', k_i * jnp.exp(g_last[..., None] - g_i)[..., None], v_new))
        return state, o

    xs = tuple(jnp.moveaxis(t, 2, 0) for t in (qc, kc, u, w, qk, gc))
    state0 = jnp.zeros((B, H, dk, dv), f32)
    _, o = lax.scan(step, state0, xs)
    return o.transpose(1, 0, 3, 2, 4).reshape(B, S, H, dv).astype(v.dtype)


def _mla(c_q, c_kv, k_pe, q_norm_c, w_uq, kv_norm_c, w_ukv, pos):
    B, S = c_q.shape[:2]
    q = (_rmsnorm(c_q, q_norm_c) @ w_uq).reshape(B, S, H_C, NOPE_C + ROPE_C)
    kv = (_rmsnorm(c_kv, kv_norm_c) @ w_ukv).reshape(B, S, H_C, NOPE_C + V_C)
    q_nope, q_pe = q[..., :NOPE_C], _rope(q[..., NOPE_C:], pos, ROPE_C)
    k_nope, v = kv[..., :NOPE_C], kv[..., NOPE_C:]
    k_pe = _rope(k_pe[:, :, None, :], pos, ROPE_C)[:, :, 0]
    scale = (NOPE_C + ROPE_C) ** -0.5

    def attend(s, e):
        sc = (jnp.einsum('bqhd,bkhd->bhqk', q_nope[:, s:e], k_nope[:, :e])
              + jnp.einsum('bqhd,bkd->bhqk', q_pe[:, s:e], k_pe[:, :e])).astype(jnp.float32) * scale
        sc = jnp.where(_causal_mask(s, e), sc, -jnp.inf)
        p = jax.nn.softmax(sc, axis=-1)
        return jnp.einsum('bhqk,bkhd->bqhd', p.astype(v.dtype), v[:, :e])

    return _causal_block_sweep(attend, S)


def _dilated_group(q, k, v, dilation, steps):
    B, S, H, dh = q.shape
    L = S // dilation
    nb = -(-L // D_BLOCK)
    Lp = nb * D_BLOCK

    def sub(t):
        t = t.reshape(B, L, dilation, H, dh).transpose(0, 2, 1, 3, 4)
        t = jnp.pad(t, ((0, 0), (0, 0), (0, Lp - L), (0, 0), (0, 0)))
        return t.reshape(B, dilation, nb, D_BLOCK, H, dh)

    def with_prev(t):
        prev = jnp.pad(t, ((0, 0), (0, 0), (1, 0), (0, 0), (0, 0), (0, 0)))[:, :, :-1]
        return jnp.concatenate([prev, t], axis=3)

    qs, kk, vv = sub(q), with_prev(sub(k)), with_prev(sub(v))
    a = jnp.arange(D_BLOCK)[:, None]
    b = jnp.arange(2 * D_BLOCK)[None, :]
    rel = D_BLOCK + a - b
    blk = jnp.arange(nb)[:, None, None]
    valid = (rel >= 0) & (rel <= steps) & ((blk > 0) | (b >= D_BLOCK))
    sc = jnp.einsum('bdnqhe,bdnkhe->bdnhqk', qs, kk).astype(jnp.float32) * (dh ** -0.5)
    sc = jnp.where(valid[:, None], sc, -jnp.inf)
    lse = jax.nn.logsumexp(sc, axis=-1)
    p = jnp.exp(sc - lse[..., None])
    o = jnp.einsum('bdnhqk,bdnkhe->bdnqhe', p.astype(vv.dtype), vv)

    def unsub(t):
        t = t.reshape(B, dilation, Lp, *t.shape[4:])[:, :, :L]
        return t.swapaxes(1, 2).reshape(B, S, *t.shape[3:])

    return unsub(o), unsub(jnp.swapaxes(lse, -1, -2)[..., None])[..., 0]


def _dilated_attention(q, k, v, pos):
    rot = DH_D // ROPE_FRACTION
    outs, lses = [], []
    for gi, (window, dil) in enumerate(DIL_GROUPS):
        o, l = _dilated_group(_rope(q[:, :, gi], pos, rot), _rope(k[:, :, gi], pos, rot),
                              v[:, :, gi], dil, window // dil)
        outs.append(o)
        lses.append(l)
    wts = jax.nn.softmax(jnp.stack(lses, -1), axis=-1)
    return jnp.einsum('bshg,gbshe->bshe', wts.astype(outs[0].dtype), jnp.stack(outs, 0))


def _mixer_sublayer(h, pos, layer, w_in, conv_b, a_log, dt_bias, out_norm_b,
                    lambda_q1, lambda_k1, lambda_q2, lambda_k2, subln_g,
                    q_norm_c, w_uq, kv_norm_c, w_ukv, w_br, w_out):
    B, S, _ = h.shape
    f32 = jnp.float32
    offsets = [int(o) for o in np.cumsum(IN_SPLITS)[:-1]]
    (q_a, k_a, v_a, q_b, k_b, v_b, beta_b, decay_b, cq_c, ckv_c, kpe_c,
     qkv_d, z, merge) = jnp.split(h @ w_in, offsets, axis=-1)

    lam_init = 0.8 - 0.6 * math.exp(-0.3 * layer)
    lam = (jnp.exp(jnp.sum(lambda_q1.astype(f32) * lambda_k1.astype(f32)))
           - jnp.exp(jnp.sum(lambda_q2.astype(f32) * lambda_k2.astype(f32))) + lam_init)
    o_a = _diff_attention(q_a.reshape(B, S, H_A, 2, DH_A), k_a.reshape(B, S, H_A, 2, DH_A),
                          v_a.reshape(B, S, H_A, 2 * DH_A), lam, pos)
    o_a = (_rmsnorm(o_a, subln_g) * (1.0 - lam_init)).reshape(B, S, BRANCH_W)

    qkv_b = jax.nn.silu(_short_conv(jnp.concatenate([q_b, k_b, v_b], -1), conv_b))
    q_b, k_b, v_b = jnp.split(qkv_b, [H_B * DK_B, 2 * H_B * DK_B], axis=-1)
    beta = jax.nn.sigmoid(beta_b.astype(f32))
    g = -jnp.exp(a_log.astype(f32)) * jax.nn.softplus(decay_b.astype(f32) + dt_bias.astype(f32))
    o_b = _gated_delta_rule(_l2norm(q_b.reshape(B, S, H_B, DK_B)), _l2norm(k_b.reshape(B, S, H_B, DK_B)),
                            v_b.reshape(B, S, H_B, DV_B), g, beta)
    o_b = _rmsnorm(o_b, out_norm_b).reshape(B, S, BRANCH_W)

    o_c = _mla(cq_c, ckv_c, kpe_c, q_norm_c, w_uq, kv_norm_c, w_ukv, pos).reshape(B, S, BRANCH_W)

    qkv_d = qkv_d.reshape(B, S, 3, N_DIL, H_D, DH_D)
    o_d = _dilated_attention(qkv_d[:, :, 0], qkv_d[:, :, 1], qkv_d[:, :, 2], pos).reshape(B, S, BRANCH_W)

    branches = jnp.stack([o_a, o_b, o_c, o_d], axis=2) * jax.nn.silu(z).reshape(B, S, N_BRANCH, BRANCH_W)
    gates = jax.nn.sigmoid(merge).reshape(B, S, N_BRANCH, D_MODEL)
    merged = jnp.einsum('bsnc,ncd,bsnd->bsd', branches, w_br, gates)
    return merged @ w_out


def setup_inputs(seed: int = 0) -> dict:
    key = jax.random.key(seed)
    ks = jax.random.split(key, 24)
    f32 = jnp.float32
    L = DEPTH

    def nrm(k, shape, scale):
        return jax.random.normal(k, shape, f32) * scale

    def gain(k, shape):
        return 1.0 + 0.02 * jax.random.normal(k, shape, f32)

    x = nrm(ks[0], (BATCH, SEQ, D_MODEL), 1.0)
    c = nrm(ks[1], (BATCH, D_MODEL), 1.0)
    offs = jax.random.randint(ks[2], (BATCH, 1), 0, 4096, dtype=jnp.int32)
    positions = offs + jnp.arange(SEQ, dtype=jnp.int32)[None, :]
    w_ada = nrm(ks[3], (L, D_MODEL, 3 * D_MODEL), D_MODEL ** -0.5)
    b_ada = nrm(ks[4], (L, 3 * D_MODEL), 0.02)
    w_in = nrm(ks[5], (L, D_MODEL, D_IN), D_MODEL ** -0.5)
    conv_b = nrm(ks[6], (L, CONV_B, 2 * H_B * DK_B + H_B * DV_B), CONV_B ** -0.5)
    a_log = jnp.log(jax.random.uniform(ks[7], (L, H_B), f32, 1.0, 16.0))
    dt = jnp.exp(jax.random.uniform(ks[8], (L, H_B), f32, math.log(1e-3), math.log(1e-1)))
    dt_bias = dt + jnp.log(-jnp.expm1(-dt))
    out_norm_b = gain(ks[9], (L, DV_B))
    lambda_q1 = nrm(ks[10], (L, DH_A), 0.1)
    lambda_k1 = nrm(ks[11], (L, DH_A), 0.1)
    lambda_q2 = nrm(ks[12], (L, DH_A), 0.1)
    lambda_k2 = nrm(ks[13], (L, DH_A), 0.1)
    subln_g = gain(ks[14], (L, 2 * DH_A))
    q_norm_c = gain(ks[15], (L, Q_LORA_C))
    w_uq = nrm(ks[16], (L, Q_LORA_C, H_C * (NOPE_C + ROPE_C)), Q_LORA_C ** -0.5)
    kv_norm_c = gain(ks[17], (L, KV_LORA_C))
    w_ukv = nrm(ks[18], (L, KV_LORA_C, H_C * (NOPE_C + V_C)), KV_LORA_C ** -0.5)
    w_br = nrm(ks[19], (L, N_BRANCH, BRANCH_W, D_MODEL), DEEPNORM_BETA * BRANCH_W ** -0.5)
    w_out = nrm(ks[20], (L, D_MODEL, D_MODEL), DEEPNORM_BETA * D_MODEL ** -0.5)
    ln_g = gain(ks[21], (L, D_MODEL))
    ln_b = nrm(ks[22], (L, D_MODEL), 0.02)
    return {'x': x, 'c': c, 'positions': positions, 'w_ada': w_ada, 'b_ada': b_ada,
            'w_in': w_in, 'conv_b': conv_b, 'a_log': a_log, 'dt_bias': dt_bias,
            'out_norm_b': out_norm_b, 'lambda_q1': lambda_q1, 'lambda_k1': lambda_k1,
            'lambda_q2': lambda_q2, 'lambda_k2': lambda_k2, 'subln_g': subln_g,
            'q_norm_c': q_norm_c, 'w_uq': w_uq, 'kv_norm_c': kv_norm_c, 'w_ukv': w_ukv,
            'w_br': w_br, 'w_out': w_out, 'ln_g': ln_g, 'ln_b': ln_b}


def reference(x, c, positions, w_ada, b_ada, w_in, conv_b, a_log, dt_bias, out_norm_b,
              lambda_q1, lambda_k1, lambda_q2, lambda_k2, subln_g, q_norm_c, w_uq,
              kv_norm_c, w_ukv, w_br, w_out, ln_g, ln_b):
    c_act = jax.nn.silu(c)
    for l in range(DEPTH):
        shift, scale, gate = jnp.split(c_act @ w_ada[l] + b_ada[l], 3, axis=-1)
        h = _layernorm(x) * (1.0 + scale[:, None, :]) + shift[:, None, :]
        y = _mixer_sublayer(h, positions, l, w_in[l], conv_b[l], a_log[l], dt_bias[l], out_norm_b[l],
                            lambda_q1[l], lambda_k1[l], lambda_q2[l], lambda_k2[l], subln_g[l],
                            q_norm_c[l], w_uq[l], kv_norm_c[l], w_ukv[l], w_br[l], w_out[l])
        x = _layernorm(DEEPNORM_ALPHA * x + gate[:, None, :] * y) * ln_g[l] + ln_b[l]
    return x
```

```python
import functools
import math

import jax
import jax.numpy as jnp
from jax import lax
from jax.experimental import pallas as pl
from jax.experimental.pallas import tpu as pltpu

F32 = jnp.float32
BF16 = jnp.bfloat16

D_MODEL = 1024
SEQ = 2048
DEPTH = 2
ROPE_THETA = 500000.0
NORM_EPS = 1e-6
H_A, DH_A = 4, 64
H_B, DK_B, DV_B, CONV_B, CHUNK_B = 4, 128, 128, 4, 64
H_C, Q_LORA_C, KV_LORA_C, NOPE_C, ROPE_C, V_C = 8, 256, 128, 64, 32, 64
H_D, DH_D = 8, 64
DIL_GROUPS = ((128, 1), (512, 4), (2048, 16))
N_BRANCH, BRANCH_W = 4, 512
DEEPNORM_ALPHA = (2.0 * DEPTH) ** 0.25

LANE = 128
VMEM_LIMIT = 56 * 1024 * 1024

_OFF_A = 0
_OFF_BQKV = 1536
_OFF_BETA = 3072
_OFF_DECAY = 3076
_OFF_CQ = 3080
_OFF_CKV = 3336
_OFF_KPE = 3464
_OFF_D = 3496
_OFF_Z = 8104
_OFF_MERGE = 10152
D_IN = 14248

PN_A = 0
PN_B = 1536
PN_CQ = 3072
PN_CKV = 3328
PN_MISC = 3456
PN_D1 = 3584
PN_Z = 5120
PN_MERGE = 7168
PN_W = 11264
MISC_BETA, MISC_DECAY, MISC_KPE = 0, 4, 96


def _cparams(n_grid):
    return pltpu.CompilerParams(dimension_semantics=("arbitrary",) * n_grid,
                                vmem_limit_bytes=VMEM_LIMIT)


def _nt(a, b):
    return lax.dot_general(a, b, (((1,), (1,)), ((), ())), preferred_element_type=F32)


def _tn(a, b):
    return lax.dot_general(a, b, (((0,), (0,)), ((), ())), preferred_element_type=F32)


def _dot(a, b):
    return jnp.dot(a, b, preferred_element_type=F32)


def _sigmoid(x):
    return 1.0 / (1.0 + jnp.exp(-x))


def _silu(x):
    return x * _sigmoid(x)


def _ada_kernel(c_ref, w_ref, b_ref, o_ref):
    ca = _silu(c_ref[...]).astype(BF16)
    o_ref[0] = _dot(ca, w_ref[0].astype(BF16)) + b_ref[0]


def ada_modulation(c, w_ada, b_ada):
    B = c.shape[0]
    L = w_ada.shape[0]
    out = pl.pallas_call(
        _ada_kernel,
        grid=(L, 3),
        in_specs=[pl.BlockSpec((B, D_MODEL), lambda l, j: (0, 0)),
                  pl.BlockSpec((1, D_MODEL, D_MODEL), lambda l, j: (l, 0, j)),
                  pl.BlockSpec((1, 1, D_MODEL), lambda l, j: (l, 0, j))],
        out_specs=pl.BlockSpec((1, B, D_MODEL), lambda l, j: (l, 0, j)),
        out_shape=jax.ShapeDtypeStruct((L, B, 3 * D_MODEL), F32),
        compiler_params=_cparams(2),
        name="ada_modulation",
    )(c, w_ada, b_ada.reshape(L, 1, 3 * D_MODEL))
    return out.reshape(L, B, 3, D_MODEL)


def _rope_rows():
    half_a = (DH_A // 4) // 2
    inv_a = ROPE_THETA ** (-jnp.arange(half_a, dtype=F32) / half_a)
    half_c = ROPE_C // 2
    inv_c = ROPE_THETA ** (-jnp.arange(half_c, dtype=F32) / half_c)
    z = jnp.zeros
    inv = jnp.concatenate([inv_a, inv_a, z(48, F32), inv_a, inv_a, z(16, F32), inv_c, inv_c])
    sgn = jnp.concatenate([-jnp.ones(8, F32), jnp.ones(8, F32), z(48, F32),
                           -jnp.ones(8, F32), jnp.ones(8, F32), z(16, F32),
                           -jnp.ones(16, F32), jnp.ones(16, F32)])
    return inv.reshape(1, LANE), sgn.reshape(1, LANE)


def _rope_table_kernel(pos_ref, inv_ref, sgn_ref, cos_a, sin_a, cos_c, sin_c):
    R = 256

    def body(i, _):
        r0 = pl.multiple_of(i * R, R)
        ang = pos_ref[pl.ds(r0, R), :].astype(F32) * inv_ref[...]
        cv = jnp.cos(ang)
        sv = jnp.sin(ang) * sgn_ref[...]
        lane = lax.broadcasted_iota(jnp.int32, (R, LANE), 1)
        is_c = lane >= MISC_KPE
        cos_a[pl.ds(r0, R), :] = jnp.where(is_c, 1.0, cv)
        sin_a[pl.ds(r0, R), :] = jnp.where(is_c, 0.0, sv)
        cos_c[pl.ds(r0, R), :] = jnp.where(is_c, cv, 1.0)
        sin_c[pl.ds(r0, R), :] = jnp.where(is_c, sv, 0.0)
        return 0

    lax.fori_loop(0, SEQ // R, body, 0)


def rope_tables(positions):
    B, S = positions.shape
    inv, sgn = _rope_rows()
    tab = jax.ShapeDtypeStruct((B * S, LANE), F32)
    row = pl.BlockSpec((1, LANE), lambda b: (0, 0))
    blk = pl.BlockSpec((S, LANE), lambda b: (b, 0))
    return pl.pallas_call(
        _rope_table_kernel,
        grid=(B,),
        in_specs=[pl.BlockSpec((S, 1), lambda b: (b, 0)), row, row],
        out_specs=[blk, blk, blk, blk],
        out_shape=[tab, tab, tab, tab],
        compiler_params=_cparams(1),
        name="rope_tables",
    )(positions.reshape(B * S, 1), inv, sgn)


def _inproj_kernel(x_ref, mod_ref, w_ref, o_ref, h_ref, *, nslab, rows):
    @pl.when(pl.program_id(1) == 0)
    def _():
        shift = mod_ref[0, 0:1, :]
        scale1 = 1.0 + mod_ref[0, 1:2, :]
        R = 128
        for s in range(nslab):
            def body(i, _):
                r0 = pl.multiple_of(i * R, R)
                xs = x_ref[0, pl.ds(r0, R), s * D_MODEL:(s + 1) * D_MODEL]
                mu = jnp.mean(xs, axis=-1, keepdims=True)
                xc = xs - mu
                var = jnp.mean(xc * xc, axis=-1, keepdims=True)
                hn = xc * lax.rsqrt(var + NORM_EPS)
                h_ref[pl.ds(s * rows + r0, R), :] = (hn * scale1 + shift).astype(BF16)
                return 0
            lax.fori_loop(0, rows // R, body, 0)

    o_ref[...] = _dot(h_ref[...], w_ref[...]).astype(BF16)


def input_projection(x3, mod, w, *, nslab, rows, steps_per_batch, tn):
    G = x3.shape[0] // 1
    n_i = x3.shape[0] * (x3.shape[2] // (nslab * D_MODEL))
    ncol = w.shape[1]
    tm = rows * nslab
    per_g = x3.shape[2] // (nslab * D_MODEL)

    return pl.pallas_call(
        functools.partial(_inproj_kernel, nslab=nslab, rows=rows),
        grid=(n_i, ncol // tn),
        in_specs=[pl.BlockSpec((1, rows, nslab * D_MODEL), lambda i, j: (i // per_g, 0, i % per_g)),
                  pl.BlockSpec((1, 3, D_MODEL), lambda i, j: (i // steps_per_batch, 0, 0)),
                  pl.BlockSpec((D_MODEL, tn), lambda i, j: (0, j))],
        out_specs=pl.BlockSpec((tm, tn), lambda i, j: (i, j)),
        out_shape=jax.ShapeDtypeStruct((n_i * tm, ncol), BF16),
        scratch_shapes=[pltpu.VMEM((tm, D_MODEL), BF16)],
        compiler_params=_cparams(2),
        name=f"input_projection_s{nslab}_r{rows}",
    )(x3, mod, w)


def _rope64(x, cos, sin, lane):
    sw = jnp.where((lane & 63) < 8, pltpu.roll(x, LANE - 8, 1), pltpu.roll(x, 8, 1))
    return x * cos + sw * sin


def _softmax_step(s, m, l, acc, vb):
    m_new = jnp.maximum(m, jnp.max(s, axis=-1, keepdims=True))
    p = jnp.exp(s - m_new)
    alpha = jnp.exp(m - m_new)
    l = alpha * l + jnp.sum(p, axis=-1, keepdims=True)
    acc = alpha * acc + _dot(p.astype(BF16), vb)
    return m_new, l, acc


TQ_A = 256


def _mixa_kernel(q_ref, k_ref, v_ref, cos_ref, sin_ref, lam_ref, g_ref, o_ref, q2_ref, kr_ref, *, lam_init):
    S = SEQ
    R = 256

    def prep(i, _):
        r0 = pl.multiple_of(i * R, R)
        lane = lax.broadcasted_iota(jnp.int32, (R, LANE), 1)
        cos = cos_ref[pl.ds(r0, R), :]
        sin = sin_ref[pl.ds(r0, R), :]
        q = _rope64(q_ref[pl.ds(r0, R), :].astype(F32), cos, sin, lane) * (DH_A ** -0.5)
        q2_ref[0, pl.ds(r0, R), :] = jnp.where(lane < 64, q, 0.0).astype(BF16)
        q2_ref[1, pl.ds(r0, R), :] = jnp.where(lane >= 64, q, 0.0).astype(BF16)
        kr_ref[pl.ds(r0, R), :] = _rope64(k_ref[pl.ds(r0, R), :].astype(F32), cos, sin, lane).astype(BF16)
        return 0

    lax.fori_loop(0, S // R, prep, 0)

    lp = lam_ref[...]
    lam = (jnp.exp(jnp.sum(lp[0:1] * lp[1:2], axis=-1, keepdims=True))
           - jnp.exp(jnp.sum(lp[2:3] * lp[3:4], axis=-1, keepdims=True)) + lam_init)
    T = TQ_A
    row = lax.broadcasted_iota(jnp.int32, (2 * T, T), 0)
    col = lax.broadcasted_iota(jnp.int32, (2 * T, T), 1)
    causal = jnp.where(row >= T, row - T, row) >= col

    def qblock(qi, _):
        q0 = pl.multiple_of(qi * T, T)
        qs = jnp.concatenate([q2_ref[0, pl.ds(q0, T), :], q2_ref[1, pl.ds(q0, T), :]], axis=0)

        def step(kj, carry, masked):
            k0 = pl.multiple_of(kj * T, T)
            s = _nt(qs, kr_ref[pl.ds(k0, T), :])
            if masked:
                s = jnp.where(causal, s, -jnp.inf)
            return _softmax_step(s, *carry, v_ref[pl.ds(k0, T), :])

        carry = (jnp.full((2 * T, 1), -jnp.inf, F32), jnp.zeros((2 * T, 1), F32), jnp.zeros((2 * T, LANE), F32))
        carry = lax.fori_loop(0, qi, lambda kj, c: step(kj, c, False), carry)
        _, l, acc = step(qi, carry, True)
        o = acc / l
        o = o[:T] - lam * o[T:]
        ms = jnp.mean(o * o, axis=-1, keepdims=True)
        o = o * lax.rsqrt(ms + NORM_EPS) * g_ref[...] * (1.0 - lam_init)
        o_ref[pl.ds(q0, T), :] = o.astype(BF16)
        return 0

    lax.fori_loop(0, S // T, qblock, 0)


def mixer_a(pn, cos_a, sin_a, lam_params, subln_g, *, lam_init=0.2):
    B = pn.shape[0] // SEQ
    S = SEQ
    col = lambda off: (lambda b, h: (b, off // LANE + h))
    blk = lambda off: pl.BlockSpec((S, LANE), col(off))
    tab = pl.BlockSpec((S, LANE), lambda b, h: (b, 0))
    return pl.pallas_call(
        functools.partial(_mixa_kernel, lam_init=lam_init),
        grid=(B, H_A),
        in_specs=[blk(PN_A), blk(PN_A + 512), blk(PN_A + 1024), tab, tab,
                  pl.BlockSpec((4, DH_A), lambda b, h: (0, 0)),
                  pl.BlockSpec((1, LANE), lambda b, h: (0, 0))],
        out_specs=pl.BlockSpec((S, LANE), lambda b, h: (b, h)),
        out_shape=jax.ShapeDtypeStruct((B * S, BRANCH_W), BF16),
        scratch_shapes=[pltpu.VMEM((2, S, LANE), BF16), pltpu.VMEM((S, LANE), BF16)],
        compiler_params=_cparams(2),
        name="mixer_a",
    )(pn, pn, pn, cos_a, sin_a, lam_params, subln_g)


TQ_C = 256


def _rmsnorm_rows(x, g):
    return x * lax.rsqrt(jnp.mean(x * x, axis=-1, keepdims=True) + NORM_EPS) * g


def _rope_c(x, cos, sin, lane):
    sw = jnp.where(lane < 112, pltpu.roll(x, LANE - 16, 1), pltpu.roll(x, 16, 1))
    return x * cos + sw * sin


def _mixc_kernel(cq_ref, ckv_ref, misc_ref, cos_ref, sin_ref, qg_ref, kvg_ref, wuq_ref, wuk_ref, wuv_ref,
                 o_ref, q_scr, k_scr, v_scr):
    S = SEQ
    R = 256
    scale = (NOPE_C + ROPE_C) ** -0.5

    def prep(i, _):
        r0 = pl.multiple_of(i * R, R)
        lane = lax.broadcasted_iota(jnp.int32, (R, LANE), 1)
        cos = cos_ref[pl.ds(r0, R), :]
        sin = sin_ref[pl.ds(r0, R), :]
        cqn = _rmsnorm_rows(cq_ref[pl.ds(r0, R), :].astype(F32), qg_ref[...]).astype(BF16)
        q = _dot(cqn, wuq_ref[...])
        kvn = _rmsnorm_rows(ckv_ref[pl.ds(r0, R), :].astype(F32), kvg_ref[...]).astype(BF16)
        kk = _dot(kvn, wuk_ref[...])
        kpe = jnp.where(lane >= MISC_KPE, misc_ref[pl.ds(r0, R), :].astype(F32), 0.0)
        kpe = _rope_c(kpe, cos, sin, lane)
        for e in range(2):
            qe = _rope_c(q[:, e * LANE:(e + 1) * LANE], cos, sin, lane) * scale
            q_scr[e, pl.ds(r0, R), :] = qe.astype(BF16)
            k_scr[e, pl.ds(r0, R), :] = (kk[:, e * LANE:(e + 1) * LANE] + kpe).astype(BF16)
        v_scr[pl.ds(r0, R), :] = _dot(kvn, wuv_ref[...]).astype(BF16)
        return 0

    lax.fori_loop(0, S // R, prep, 0)

    T = TQ_C
    row = lax.broadcasted_iota(jnp.int32, (2 * T, T), 0)
    col = lax.broadcasted_iota(jnp.int32, (2 * T, T), 1)
    causal = jnp.where(row >= T, row - T, row) >= col
    lane_o = lax.broadcasted_iota(jnp.int32, (T, LANE), 1)

    def qblock(qi, _):
        q0 = pl.multiple_of(qi * T, T)
        qa = q_scr[0, pl.ds(q0, T), :]
        qb = q_scr[1, pl.ds(q0, T), :]

        def step(kj, carry, masked):
            k0 = pl.multiple_of(kj * T, T)
            s = jnp.concatenate([_nt(qa, k_scr[0, pl.ds(k0, T), :]), _nt(qb, k_scr[1, pl.ds(k0, T), :])], axis=0)
            if masked:
                s = jnp.where(causal, s, -jnp.inf)
            return _softmax_step(s, *carry, v_scr[pl.ds(k0, T), :])

        carry = (jnp.full((2 * T, 1), -jnp.inf, F32), jnp.zeros((2 * T, 1), F32), jnp.zeros((2 * T, LANE), F32))
        carry = lax.fori_loop(0, qi, lambda kj, c: step(kj, c, False), carry)
        _, l, acc = step(qi, carry, True)
        o = acc / l
        o_ref[pl.ds(q0, T), :] = jnp.where(lane_o < 64, o[:T], o[T:]).astype(BF16)
        return 0

    lax.fori_loop(0, S // T, qblock, 0)


def mixer_c(pn, cos_c, sin_c, q_norm, kv_norm, wuq, wuk, wuv):
    B = pn.shape[0] // SEQ
    S = SEQ
    tab = pl.BlockSpec((S, LANE), lambda b, p: (b, 0))
    return pl.pallas_call(
        _mixc_kernel,
        grid=(B, H_C // 2),
        in_specs=[pl.BlockSpec((S, Q_LORA_C), lambda b, p: (b, PN_CQ // Q_LORA_C)),
                  pl.BlockSpec((S, LANE), lambda b, p: (b, PN_CKV // LANE)),
                  pl.BlockSpec((S, LANE), lambda b, p: (b, PN_MISC // LANE)),
                  tab, tab,
                  pl.BlockSpec((1, Q_LORA_C), lambda b, p: (0, 0)),
                  pl.BlockSpec((1, KV_LORA_C), lambda b, p: (0, 0)),
                  pl.BlockSpec((Q_LORA_C, 2 * LANE), lambda b, p: (0, p)),
                  pl.BlockSpec((KV_LORA_C, 2 * LANE), lambda b, p: (0, p)),
                  pl.BlockSpec((KV_LORA_C, LANE), lambda b, p: (0, p))],
        out_specs=pl.BlockSpec((S, LANE), lambda b, p: (b, p)),
        out_shape=jax.ShapeDtypeStruct((B * S, BRANCH_W), BF16),
        scratch_shapes=[pltpu.VMEM((2, S, LANE), BF16), pltpu.VMEM((2, S, LANE), BF16), pltpu.VMEM((S, LANE), BF16)],
        compiler_params=_cparams(2),
        name="mixer_c",
    )(pn, pn, pn, cos_c, sin_c, q_norm, kv_norm, wuq, wuk, wuv)


DB = 128
LSE_W = 16


def _mixd_kernel(*refs, nblk, combine):
    if combine:
        q_ref, k_ref, v_ref, cos_ref, sin_ref, o2_ref, o3_ref, l2_ref, l3_ref, o_ref, q2_scr, kr_scr = refs
    else:
        q_ref, k_ref, v_ref, cos_ref, sin_ref, o_ref, lse_ref, q2_scr, kr_scr = refs
    lane = lax.broadcasted_iota(jnp.int32, (DB, LANE), 1)

    def prep(i, _):
        r0 = pl.multiple_of(i * DB, DB)
        cos = cos_ref[0, pl.ds(r0, DB), :]
        sin = sin_ref[0, pl.ds(r0, DB), :]
        for p in range(4):
            cs = slice(p * LANE, (p + 1) * LANE)
            q = _rope64(q_ref[0, pl.ds(r0, DB), cs].astype(F32), cos, sin, lane) * (DH_D ** -0.5)
            q2_scr[0, pl.ds(r0, DB), cs] = jnp.where(lane < 64, q, 0.0).astype(BF16)
            q2_scr[1, pl.ds(r0, DB), cs] = jnp.where(lane >= 64, q, 0.0).astype(BF16)
            kr_scr[pl.ds(r0, DB), cs] = _rope64(k_ref[0, pl.ds(r0, DB), cs].astype(F32), cos, sin, lane).astype(BF16)
        return 0

    lax.fori_loop(0, nblk, prep, 0)

    def attend(j, first):
        r0 = pl.multiple_of(j * DB, DB)
        nk = DB if first else 2 * DB
        k0 = 0 if first else pl.multiple_of(r0 - DB, DB)
        a = lax.broadcasted_iota(jnp.int32, (2 * DB, nk), 0) & (DB - 1)
        b = lax.broadcasted_iota(jnp.int32, (2 * DB, nk), 1)
        valid = (b <= a) if first else ((b >= a) & (b <= a + DB))
        lse_tile = jnp.zeros((DB, LANE), F32)
        if combine:
            l2t = l2_ref[0, pl.ds(r0, DB), :]
            l3t = l3_ref[0, pl.ds(r0, DB), :]
        for p in range(4):
            cs = slice(p * LANE, (p + 1) * LANE)
            qs = jnp.concatenate([q2_scr[0, pl.ds(r0, DB), cs], q2_scr[1, pl.ds(r0, DB), cs]], axis=0)
            s = jnp.where(valid, _nt(qs, kr_scr[pl.ds(k0, nk), cs]), -jnp.inf)
            m = jnp.max(s, axis=-1, keepdims=True)
            pr = jnp.exp(s - m)
            l = jnp.sum(pr, axis=-1, keepdims=True)
            o = _dot(pr.astype(BF16), v_ref[0, pl.ds(k0, nk), cs]) / l
            lse = m + jnp.log(l)
            o1 = jnp.where(lane < 64, o[:DB], o[DB:])
            if combine:
                c0 = 2 * p * LSE_W
                c1 = c0 + LSE_W
                L1 = jnp.where(lane < 64, lse[:DB], lse[DB:])
                L2 = jnp.where(lane < 64, l2t[:, c0:c0 + 1], l2t[:, c1:c1 + 1])
                L3 = jnp.where(lane < 64, l3t[:, c0:c0 + 1], l3t[:, c1:c1 + 1])
                mx = jnp.maximum(L1, jnp.maximum(L2, L3))
                w1 = jnp.exp(L1 - mx)
                w2 = jnp.exp(L2 - mx)
                w3 = jnp.exp(L3 - mx)
                o2 = o2_ref[0, pl.ds(r0, DB), cs].astype(F32)
                o3 = o3_ref[0, pl.ds(r0, DB), cs].astype(F32)
                o1 = (w1 * o1 + w2 * o2 + w3 * o3) / (w1 + w2 + w3)
            else:
                hd = lane // LSE_W
                lse_tile = jnp.where(hd == 2 * p, lse[:DB], jnp.where(hd == 2 * p + 1, lse[DB:], lse_tile))
            o_ref[0, pl.ds(r0, DB), cs] = o1.astype(BF16)
        if not combine:
            lse_ref[0, pl.ds(r0, DB), :] = lse_tile

    attend(0, True)
    if nblk > 1:
        def body(j, _):
            attend(j, False)
            return 0
        lax.fori_loop(1, nblk, body, 0)


def _mixd_call(qkv3, qcol0, cos3, sin3, prev, *, dil, B):
    R = SEQ // dil
    nblk = R // DB
    combine = prev is not None
    qspec = lambda c: pl.BlockSpec((1, R, BRANCH_W), lambda b, r: (b * dil + r, 0, qcol0 + c))
    nat = lambda w: pl.BlockSpec((1, R, w), lambda b, r: (b, 0, r))
    in_specs = [qspec(0), qspec(1), qspec(2), nat(LANE), nat(LANE)]
    args = [qkv3, qkv3, qkv3, cos3, sin3]
    o_sds = jax.ShapeDtypeStruct((B, R, dil * BRANCH_W), BF16)
    if combine:
        in_specs += [nat(BRANCH_W), nat(BRANCH_W), nat(LANE), nat(LANE)]
        args += list(prev)
        out_specs, out_shape = nat(BRANCH_W), o_sds
    else:
        out_specs = [nat(BRANCH_W), nat(LANE)]
        out_shape = [o_sds, jax.ShapeDtypeStruct((B, R, dil * LANE), F32)]
    return pl.pallas_call(
        functools.partial(_mixd_kernel, nblk=nblk, combine=combine),
        grid=(B, dil),
        in_specs=in_specs,
        out_specs=out_specs,
        out_shape=out_shape,
        scratch_shapes=[pltpu.VMEM((2, R, BRANCH_W), BF16), pltpu.VMEM((R, BRANCH_W), BF16)],
        compiler_params=_cparams(2),
        name=f"mixer_d_dil{dil}",
    )(*args)


def mixer_d(pn, p2, p3, cos_a, sin_a):
    B = pn.shape[0] // SEQ
    S = SEQ
    o3, l3 = _mixd_call(p3.reshape(B * 16, S // 16, 1536), 0, cos_a.reshape(B, S // 16, 16 * LANE),
                        sin_a.reshape(B, S // 16, 16 * LANE), None, dil=16, B=B)
    o2, l2 = _mixd_call(p2.reshape(B * 4, S // 4, 1536), 0, cos_a.reshape(B, S // 4, 4 * LANE),
                        sin_a.reshape(B, S // 4, 4 * LANE), None, dil=4, B=B)
    prev = (o2.reshape(B, S, BRANCH_W), o3.reshape(B, S, BRANCH_W), l2.reshape(B, S, LANE), l3.reshape(B, S, LANE))
    o = _mixd_call(pn.reshape(B, S, PN_W), PN_D1 // BRANCH_W, cos_a.reshape(B, S, LANE), sin_a.reshape(B, S, LANE),
                   prev, dil=1, B=B)
    return o.reshape(B * S, BRANCH_W)


GB = 256
NCHUNK = SEQ // CHUNK_B
PAD_B = 8


def _split3(x):
    hi = x.astype(BF16)
    r1 = x - hi.astype(F32)
    mid = r1.astype(BF16)
    lo = (r1 - mid.astype(F32)).astype(BF16)
    return hi, mid, lo


def _mixb_kernel(qkv_ref, misc_ref, conv_ref, alog_ref, dt_ref, ng_ref, o_ref,
                 xpad, c_scr, gc_scr, gl_scr, gct_scr, qeff_scr, o0_scr, n_scr, g_scr, s_scr):
    S = SEQ
    R = GB
    NG = S // R
    ri = lax.broadcasted_iota(jnp.int32, (R, R), 0)
    ci = lax.broadcasted_iota(jnp.int32, (R, R), 1)
    same = (ri // CHUNK_B) == (ci // CHUNK_B)
    tril = same & (ri >= ci)
    strict = same & (ri > ci)
    eye = jnp.where(ri == ci, 1.0, 0.0)

    sel = jnp.concatenate([jnp.where(tril, 1.0, 0.0), jnp.where(same, 1.0, 0.0)], axis=0).astype(BF16)
    neg_a = -jnp.exp(alog_ref[...])

    def gprep(i, _):
        r0 = pl.multiple_of(i * R, R)
        x = misc_ref[pl.ds(r0, R), :].astype(F32) + dt_ref[...]
        sp = jnp.maximum(x, 0.0) + jnp.log1p(jnp.exp(-jnp.abs(x)))
        g = neg_a * sp
        hi, mid, lo = _split3(g)
        acc = _dot(sel, jnp.concatenate([hi, mid, lo], axis=1))
        acc = acc[:, 0:LANE] + acc[:, LANE:2 * LANE] + acc[:, 2 * LANE:3 * LANE]
        gc_scr[pl.ds(r0, R), :] = acc[:R]
        gl_scr[pl.ds(r0, R), :] = acc[R:]
        gct_scr[i] = acc[:R].T[0:8, :]
        return 0

    lax.fori_loop(0, NG, gprep, 0)

    xpad[0:PAD_B, :] = jnp.zeros((PAD_B, LANE), F32)
    for h in range(H_B):
        for t in range(3):
            cb = t * H_B + h
            cs = slice(cb * LANE, (cb + 1) * LANE)

            def fill(i, _):
                r0 = pl.multiple_of(i * R, R)
                xpad[pl.ds(PAD_B + r0, R), :] = qkv_ref[pl.ds(r0, R), cs].astype(F32)
                return 0

            lax.fori_loop(0, NG, fill, 0)

            def conv(i, _):
                r0 = pl.multiple_of(i * R, R)
                y = jnp.zeros((R, LANE), F32)
                for j in range(CONV_B):
                    y = y + xpad[pl.ds(r0 + (PAD_B - (CONV_B - 1) + j), R), :] * conv_ref[j:j + 1, cs]
                y = _silu(y)
                if t < 2:
                    y = y * lax.rsqrt(jnp.sum(y * y, axis=-1, keepdims=True) + NORM_EPS)
                c_scr[pl.ds(r0, R), t * LANE:(t + 1) * LANE] = y
                return 0

            lax.fori_loop(0, NG, conv, 0)

        def group(i, _):
            r0 = pl.multiple_of(i * R, R)
            qn = c_scr[pl.ds(r0, R), 0:LANE]
            kn = c_scr[pl.ds(r0, R), LANE:2 * LANE]
            v = c_scr[pl.ds(r0, R), 2 * LANE:3 * LANE]
            mf = misc_ref[pl.ds(r0, R), :].astype(F32)
            beta = _sigmoid(mf[:, MISC_BETA + h:MISC_BETA + h + 1])
            gcol = gc_scr[pl.ds(r0, R), MISC_DECAY + h:MISC_DECAY + h + 1]
            glcol = gl_scr[pl.ds(r0, R), MISC_DECAY + h:MISC_DECAY + h + 1]
            grow = gct_scr[i, MISC_DECAY + h:MISC_DECAY + h + 1, :]
            dec = jnp.exp(jnp.where(tril, gcol - grow, 0.0))
            kb = kn * beta
            knb = kn.astype(BF16)
            lm = jnp.where(strict, _nt(kb.astype(BF16), knb) * dec, 0.0)
            qkm = jnp.where(tril, _nt((qn * (DK_B ** -0.5)).astype(BF16), knb) * dec, 0.0)
            x = eye - jnp.where((ri // 2) == (ci // 2), lm, 0.0)
            s = 2
            while s < CHUNK_B:
                off = ((ri // (2 * s)) == (ci // (2 * s))) & ((ri // s) != (ci // s))
                xb = x.astype(BF16)
                x = x - _dot(xb, _dot(jnp.where(off, lm, 0.0).astype(BF16), xb).astype(BF16))
                s *= 2
            rhs = jnp.concatenate([v * beta, kb * jnp.exp(gcol)], axis=1).astype(BF16)
            uw = _dot(x.astype(BF16), rhs)
            uwb = uw.astype(BF16)
            ox = _dot(qkm.astype(BF16), uwb)
            qeff_scr[h, pl.ds(r0, R), :] = (qn * (DK_B ** -0.5) * jnp.exp(gcol) - ox[:, LANE:]).astype(BF16)
            o0_scr[h, pl.ds(r0, R), :] = ox[:, :LANE]
            kt = (kn * jnp.exp(glcol - gcol)).astype(BF16)
            for c in range(R // CHUNK_B):
                gn = _tn(kt[c * CHUNK_B:(c + 1) * CHUNK_B], uwb[c * CHUNK_B:(c + 1) * CHUNK_B])
                n_scr[h, i * (R // CHUNK_B) + c] = gn[:, :LANE]
                g_scr[h, i * (R // CHUNK_B) + c] = gn[:, LANE:].astype(BF16)
            return 0

        lax.fori_loop(0, NG, group, 0)

    s_scr[...] = jnp.zeros(s_scr.shape, F32)

    def scan(c, _):
        r0 = pl.multiple_of(c * CHUNK_B, CHUNK_B)
        glrow = jnp.exp(gl_scr[pl.ds(r0, 1), :])
        for h in range(H_B):
            st = s_scr[h]
            sb = st.astype(BF16)
            o = _dot(qeff_scr[h, pl.ds(r0, CHUNK_B), :], sb) + o0_scr[h, pl.ds(r0, CHUNK_B), :]
            o = _rmsnorm_rows(o, ng_ref[...])
            o_ref[pl.ds(r0, CHUNK_B), h * LANE:(h + 1) * LANE] = o.astype(BF16)
            a = glrow[:, MISC_DECAY + h:MISC_DECAY + h + 1]
            s_scr[h] = a * st - _dot(g_scr[h, c], sb) + n_scr[h, c]
        return 0

    lax.fori_loop(0, NCHUNK, scan, 0)


def mixer_b(pn, conv_w, alog_row, dt_row, out_norm):
    B = pn.shape[0] // SEQ
    S = SEQ
    W = 3 * H_B * DK_B
    row = pl.BlockSpec((1, LANE), lambda b: (0, 0))
    return pl.pallas_call(
        _mixb_kernel,
        grid=(B,),
        in_specs=[pl.BlockSpec((S, W), lambda b: (b, PN_B // W)),
                  pl.BlockSpec((S, LANE), lambda b: (b, PN_MISC // LANE)),
                  pl.BlockSpec((CONV_B, W), lambda b: (0, 0)), row, row, row],
        out_specs=pl.BlockSpec((S, BRANCH_W), lambda b: (b, 0)),
        out_shape=jax.ShapeDtypeStruct((B * S, BRANCH_W), BF16),
        scratch_shapes=[pltpu.VMEM((PAD_B + S, LANE), F32),
                        pltpu.VMEM((S, 3 * LANE), F32),
                        pltpu.VMEM((S, LANE), F32),
                        pltpu.VMEM((S, LANE), F32),
                        pltpu.VMEM((S // GB, 8, GB), F32),
                        pltpu.VMEM((H_B, S, LANE), BF16),
                        pltpu.VMEM((H_B, S, LANE), F32),
                        pltpu.VMEM((H_B, NCHUNK, DK_B, DV_B), F32),
                        pltpu.VMEM((H_B, NCHUNK, DK_B, DK_B), BF16),
                        pltpu.VMEM((H_B, DK_B, DV_B), F32)],
        compiler_params=_cparams(1),
        name="mixer_b",
    )(pn, pn, conv_w, alog_row, dt_row, out_norm)


TM_MERGE = 512


def _merge_kernel(*refs):
    o_refs = refs[0:4]
    z_refs = refs[4:8]
    m_refs = refs[8:12]
    x_ref, mod_ref, wbr_ref, wout_ref, lng_ref, lnb_ref, out_ref = refs[12:]
    merged = jnp.zeros((TM_MERGE, D_MODEL), F32)
    for n in range(N_BRANCH):
        br = (o_refs[n][...].astype(F32) * _silu(z_refs[n][...].astype(F32))).astype(BF16)
        merged = merged + _dot(br, wbr_ref[n]) * _sigmoid(m_refs[n][...].astype(F32))
    y = _dot(merged.astype(BF16), wout_ref[...])
    t = DEEPNORM_ALPHA * x_ref[...] + mod_ref[0, 2:3, :] * y
    mu = jnp.mean(t, axis=-1, keepdims=True)
    tc = t - mu
    var = jnp.mean(tc * tc, axis=-1, keepdims=True)
    out_ref[...] = tc * lax.rsqrt(var + NORM_EPS) * lng_ref[...] + lnb_ref[...]


def merge_out(pn, o_a, o_b, o_c, o_d, x2, mod, w_br, w_out, ln_g, ln_b):
    T = x2.shape[0]
    tm = TM_MERGE
    per_b = SEQ // tm
    br = pl.BlockSpec((tm, BRANCH_W), lambda i: (i, 0))
    zspec = lambda n: pl.BlockSpec((tm, BRANCH_W), lambda i: (i, PN_Z // BRANCH_W + n))
    mspec = lambda n: pl.BlockSpec((tm, D_MODEL), lambda i: (i, PN_MERGE // D_MODEL + n))
    row = pl.BlockSpec((1, D_MODEL), lambda i: (0, 0))
    return pl.pallas_call(
        _merge_kernel,
        grid=(T // tm,),
        in_specs=[br, br, br, br] + [zspec(n) for n in range(4)] + [mspec(n) for n in range(4)] + [
            pl.BlockSpec((tm, D_MODEL), lambda i: (i, 0)),
            pl.BlockSpec((1, 3, D_MODEL), lambda i: (i // per_b, 0, 0)),
            pl.BlockSpec((N_BRANCH, BRANCH_W, D_MODEL), lambda i: (0, 0, 0)),
            pl.BlockSpec((D_MODEL, D_MODEL), lambda i: (0, 0)), row, row],
        out_specs=pl.BlockSpec((tm, D_MODEL), lambda i: (i, 0)),
        out_shape=jax.ShapeDtypeStruct((T, D_MODEL), F32),
        compiler_params=_cparams(1),
        name="merge_out",
    )(o_a, o_b, o_c, o_d, pn, pn, pn, pn, pn, pn, pn, pn, x2, mod, w_br, w_out, ln_g, ln_b)


def _prep_w_in(w):
    d = lambda part, g: w[:, _OFF_D + part * 1536 + g * 512:_OFF_D + part * 1536 + (g + 1) * 512]
    zc = lambda n: jnp.zeros((D_MODEL, n), w.dtype)
    misc = jnp.concatenate([w[:, _OFF_BETA:_OFF_BETA + 4], w[:, _OFF_DECAY:_OFF_DECAY + 4], zc(MISC_KPE - 8),
                            w[:, _OFF_KPE:_OFF_KPE + ROPE_C]], axis=1)
    wn = jnp.concatenate([w[:, _OFF_A:_OFF_A + 1536], w[:, _OFF_BQKV:_OFF_BQKV + 1536],
                          w[:, _OFF_CQ:_OFF_CQ + Q_LORA_C], w[:, _OFF_CKV:_OFF_CKV + KV_LORA_C], misc,
                          d(0, 0), d(1, 0), d(2, 0), w[:, _OFF_Z:_OFF_Z + 2048], w[:, _OFF_MERGE:_OFF_MERGE + 4096]],
                         axis=1)
    w2 = jnp.concatenate([d(0, 1), d(1, 1), d(2, 1)], axis=1)
    w3 = jnp.concatenate([d(0, 2), d(1, 2), d(2, 2)], axis=1)
    return wn.astype(BF16), w2.astype(BF16), w3.astype(BF16)


def _prep_w_c(w_uq, w_ukv):
    q = w_uq.reshape(Q_LORA_C, H_C, NOPE_C + ROPE_C)
    wq = jnp.concatenate([q[..., :NOPE_C], jnp.zeros((Q_LORA_C, H_C, LANE - NOPE_C - ROPE_C), q.dtype),
                          q[..., NOPE_C:]], axis=-1).reshape(Q_LORA_C, H_C * LANE)
    kv = w_ukv.reshape(KV_LORA_C, H_C, NOPE_C + V_C)
    wk = jnp.concatenate([kv[..., :NOPE_C], jnp.zeros((KV_LORA_C, H_C, LANE - NOPE_C), kv.dtype)],
                         axis=-1).reshape(KV_LORA_C, H_C * LANE)
    wv = kv[..., NOPE_C:].reshape(KV_LORA_C, H_C * V_C)
    return wq.astype(BF16), wk.astype(BF16), wv.astype(BF16)


def _lane_row(vals, at):
    return jnp.zeros((1, LANE), F32).at[0, at:at + vals.shape[0]].set(vals.astype(F32))


def kernel(x, c, positions, w_ada, b_ada, w_in, conv_b, a_log, dt_bias, out_norm_b, lambda_q1, lambda_k1,
           lambda_q2, lambda_k2, subln_g, q_norm_c, w_uq, kv_norm_c, w_ukv, w_br, w_out, ln_g, ln_b):
    B, S, D = x.shape
    assert (S, D) == (SEQ, D_MODEL) and w_in.shape[0] == DEPTH
    T = B * S
    mods = ada_modulation(c, w_ada, b_ada)
    cos_a, sin_a, cos_c, sin_c = rope_tables(positions)
    x2 = x.reshape(T, D)
    for l in range(DEPTH):
        mod = mods[l]
        wn, w2, w3 = _prep_w_in(w_in[l])
        pn = input_projection(x2.reshape(T // 1024, 1024, D), mod, wn, nslab=1, rows=1024, steps_per_batch=S // 1024,
                              tn=1024)
        p2 = input_projection(x2.reshape(B, S // 4, 4 * D), mod, w2, nslab=1, rows=S // 4, steps_per_batch=4, tn=1536)
        p3 = input_projection(x2.reshape(B, S // 16, 16 * D), mod, w3, nslab=4, rows=S // 16, steps_per_batch=4,
                              tn=1536)
        lam = jnp.stack([lambda_q1[l], lambda_k1[l], lambda_q2[l], lambda_k2[l]]).astype(F32)
        o_a = mixer_a(pn, cos_a, sin_a, lam, subln_g[l].reshape(1, LANE),
                      lam_init=0.8 - 0.6 * math.exp(-0.3 * l))
        o_b = mixer_b(pn, conv_b[l], _lane_row(a_log[l], MISC_DECAY), _lane_row(dt_bias[l], MISC_DECAY),
                      out_norm_b[l].reshape(1, LANE))
        wq, wk, wv = _prep_w_c(w_uq[l], w_ukv[l])
        o_c = mixer_c(pn, cos_c, sin_c, q_norm_c[l].reshape(1, Q_LORA_C), kv_norm_c[l].reshape(1, KV_LORA_C),
                      wq, wk, wv)
        o_d = mixer_d(pn, p2, p3, cos_a, sin_a)
        x2 = merge_out(pn, o_a, o_b, o_c, o_d, x2, mod, w_br[l].astype(BF16), w_out[l].astype(BF16),
                       ln_g[l].reshape(1, D), ln_b[l].reshape(1, D))
    return x2.reshape(B, S, D)
```

```python
import functools
import math

import jax
import jax.numpy as jnp
from jax import lax
from jax.experimental import pallas as pl
from jax.experimental.pallas import tpu as pltpu

F32 = jnp.float32
BF16 = jnp.bfloat16

D_MODEL = 1024
SEQ = 2048
DEPTH = 2
ROPE_THETA = 500000.0
NORM_EPS = 1e-6
H_A, DH_A = 4, 64
H_B, DK_B, DV_B, CONV_B, CHUNK_B = 4, 128, 128, 4, 64
H_C, Q_LORA_C, KV_LORA_C, NOPE_C, ROPE_C, V_C = 8, 256, 128, 64, 32, 64
H_D, DH_D = 8, 64
DIL_GROUPS = ((128, 1), (512, 4), (2048, 16))
N_BRANCH, BRANCH_W = 4, 512
DEEPNORM_ALPHA = (2.0 * DEPTH) ** 0.25

LANE = 128
VMEM_LIMIT = 56 * 1024 * 1024

_OFF_A = 0
_OFF_BQKV = 1536
_OFF_BETA = 3072
_OFF_DECAY = 3076
_OFF_CQ = 3080
_OFF_CKV = 3336
_OFF_KPE = 3464
_OFF_D = 3496
_OFF_Z = 8104
_OFF_MERGE = 10152
D_IN = 14248

PN_A = 0
PN_B = 1536
PN_CQ = 3072
PN_CKV = 3328
PN_MISC = 3456
PN_D1 = 3584
PN_Z = 5120
PN_MERGE = 7168
PN_W = 11264
MISC_BETA, MISC_DECAY, MISC_KPE = 0, 4, 96


def _cparams(n_grid):
    return pltpu.CompilerParams(dimension_semantics=("arbitrary",) * n_grid,
                                vmem_limit_bytes=VMEM_LIMIT)


def _nt(a, b):
    return lax.dot_general(a, b, (((1,), (1,)), ((), ())), preferred_element_type=F32)


def _tn(a, b):
    return lax.dot_general(a, b, (((0,), (0,)), ((), ())), preferred_element_type=F32)


def _dot(a, b):
    return jnp.dot(a, b, preferred_element_type=F32)


def _sigmoid(x):
    return 1.0 / (1.0 + jnp.exp(-x))


def _silu(x):
    return x * _sigmoid(x)


def _ada_kernel(c_ref, w_ref, b_ref, o_ref):
    ca = _silu(c_ref[...]).astype(BF16)
    o_ref[0] = _dot(ca, w_ref[0].astype(BF16)) + b_ref[0]


def ada_modulation(c, w_ada, b_ada):
    B = c.shape[0]
    L = w_ada.shape[0]
    out = pl.pallas_call(
        _ada_kernel,
        grid=(L, 3),
        in_specs=[pl.BlockSpec((B, D_MODEL), lambda l, j: (0, 0)),
                  pl.BlockSpec((1, D_MODEL, D_MODEL), lambda l, j: (l, 0, j)),
                  pl.BlockSpec((1, 1, D_MODEL), lambda l, j: (l, 0, j))],
        out_specs=pl.BlockSpec((1, B, D_MODEL), lambda l, j: (l, 0, j)),
        out_shape=jax.ShapeDtypeStruct((L, B, 3 * D_MODEL), F32),
        compiler_params=_cparams(2),
        name="ada_modulation",
    )(c, w_ada, b_ada.reshape(L, 1, 3 * D_MODEL))
    return out.reshape(L, B, 3, D_MODEL)


def _rope_rows():
    half_a = (DH_A // 4) // 2
    inv_a = ROPE_THETA ** (-jnp.arange(half_a, dtype=F32) / half_a)
    half_c = ROPE_C // 2
    inv_c = ROPE_THETA ** (-jnp.arange(half_c, dtype=F32) / half_c)
    z = jnp.zeros
    inv = jnp.concatenate([inv_a, inv_a, z(48, F32), inv_a, inv_a, z(16, F32), inv_c, inv_c])
    sgn = jnp.concatenate([-jnp.ones(8, F32), jnp.ones(8, F32), z(48, F32),
                           -jnp.ones(8, F32), jnp.ones(8, F32), z(16, F32),
                           -jnp.ones(16, F32), jnp.ones(16, F32)])
    return inv.reshape(1, LANE), sgn.reshape(1, LANE)


def _rope_table_kernel(pos_ref, inv_ref, sgn_ref, cos_a, sin_a, cos_c, sin_c):
    R = 256

    def body(i, _):
        r0 = pl.multiple_of(i * R, R)
        ang = pos_ref[pl.ds(r0, R), :].astype(F32) * inv_ref[...]
        cv = jnp.cos(ang)
        sv = jnp.sin(ang) * sgn_ref[...]
        lane = lax.broadcasted_iota(jnp.int32, (R, LANE), 1)
        is_c = lane >= MISC_KPE
        cos_a[pl.ds(r0, R), :] = jnp.where(is_c, 1.0, cv)
        sin_a[pl.ds(r0, R), :] = jnp.where(is_c, 0.0, sv)
        cos_c[pl.ds(r0, R), :] = jnp.where(is_c, cv, 1.0)
        sin_c[pl.ds(r0, R), :] = jnp.where(is_c, sv, 0.0)
        return 0

    lax.fori_loop(0, SEQ // R, body, 0)


def rope_tables(positions):
    B, S = positions.shape
    inv, sgn = _rope_rows()
    tab = jax.ShapeDtypeStruct((B * S, LANE), F32)
    row = pl.BlockSpec((1, LANE), lambda b: (0, 0))
    blk = pl.BlockSpec((S, LANE), lambda b: (b, 0))
    return pl.pallas_call(
        _rope_table_kernel,
        grid=(B,),
        in_specs=[pl.BlockSpec((S, 1), lambda b: (b, 0)), row, row],
        out_specs=[blk, blk, blk, blk],
        out_shape=[tab, tab, tab, tab],
        compiler_params=_cparams(1),
        name="rope_tables",
    )(positions.reshape(B * S, 1), inv, sgn)


def _inproj_kernel(x_ref, mod_ref, w_ref, o_ref, h_ref, *, nslab, rows):
    @pl.when(pl.program_id(1) == 0)
    def _():
        shift = mod_ref[0, 0:1, :]
        scale1 = 1.0 + mod_ref[0, 1:2, :]
        R = 128
        for s in range(nslab):
            def body(i, _):
                r0 = pl.multiple_of(i * R, R)
                xs = x_ref[0, pl.ds(r0, R), s * D_MODEL:(s + 1) * D_MODEL]
                mu = jnp.mean(xs, axis=-1, keepdims=True)
                xc = xs - mu
                var = jnp.mean(xc * xc, axis=-1, keepdims=True)
                hn = xc * lax.rsqrt(var + NORM_EPS)
                h_ref[pl.ds(s * rows + r0, R), :] = (hn * scale1 + shift).astype(BF16)
                return 0
            lax.fori_loop(0, rows // R, body, 0)

    o_ref[...] = _dot(h_ref[...], w_ref[...]).astype(BF16)


def input_projection(x3, mod, w, *, nslab, rows, steps_per_batch, tn):
    n_i = x3.shape[0] * (x3.shape[2] // (nslab * D_MODEL))
    ncol = w.shape[1]
    tm = rows * nslab
    per_g = x3.shape[2] // (nslab * D_MODEL)

    return pl.pallas_call(
        functools.partial(_inproj_kernel, nslab=nslab, rows=rows),
        grid=(n_i, ncol // tn),
        in_specs=[pl.BlockSpec((1, rows, nslab * D_MODEL), lambda i, j: (i // per_g, 0, i % per_g)),
                  pl.BlockSpec((1, 3, D_MODEL), lambda i, j: (i // steps_per_batch, 0, 0)),
                  pl.BlockSpec((D_MODEL, tn), lambda i, j: (0, j))],
        out_specs=pl.BlockSpec((tm, tn), lambda i, j: (i, j)),
        out_shape=jax.ShapeDtypeStruct((n_i * tm, ncol), BF16),
        scratch_shapes=[pltpu.VMEM((tm, D_MODEL), BF16)],
        compiler_params=_cparams(2),
        name=f"input_projection_s{nslab}_r{rows}",
    )(x3, mod, w)


def _rope64(x, cos, sin, lane):
    sw = jnp.where((lane & 63) < 8, pltpu.roll(x, LANE - 8, 1), pltpu.roll(x, 8, 1))
    return x * cos + sw * sin


TQ = 256
TKB = 256
LOG2E = 1.4426950408889634


def _scores_t(k_ref, q_blk, qi, s_ref, chunk=512):
    kmax = (qi + 1) * TQ
    m = None
    for c0 in range(0, kmax, chunk):
        c1 = min(c0 + chunk, kmax)
        s = _nt(k_ref[c0:c1, :], q_blk)
        if c1 == kmax:
            kk = lax.broadcasted_iota(jnp.int32, (c1 - c0, TQ), 0) + (c0 - qi * TQ)
            qq = lax.broadcasted_iota(jnp.int32, (c1 - c0, TQ), 1)
            s = jnp.where(kk <= qq, s, -jnp.inf)
        s_ref[c0:c1, :] = s
        mc = jnp.max(s, axis=0, keepdims=True)
        m = mc if m is None else jnp.maximum(m, mc)
    return m


def _probs_t(s_ref, p_ref, m, ntiles):
    def body(t, l):
        r0 = pl.multiple_of(t * TKB, TKB)
        p = jnp.exp2(s_ref[pl.ds(r0, TKB), :] - m)
        p_ref[pl.ds(r0, TKB), :] = p.astype(BF16)
        return l + jnp.sum(p, axis=0, keepdims=True)
    return lax.fori_loop(0, ntiles, body, jnp.zeros((1, TQ), F32))


def _mixa_kernel(q_ref, k_ref, v_ref, cos_ref, sin_ref, lam_ref, g_ref, o_ref, q2_ref, kr_ref, vt_ref, s_ref, p_ref,
                 *, lam_init):
    S = SEQ
    R = 128

    def prep(i, _):
        r0 = pl.multiple_of(i * R, R)
        lane = lax.broadcasted_iota(jnp.int32, (R, LANE), 1)
        cos = cos_ref[pl.ds(r0, R), :]
        sin = sin_ref[pl.ds(r0, R), :]
        q = _rope64(q_ref[pl.ds(r0, R), :].astype(F32), cos, sin, lane) * (DH_A ** -0.5 * LOG2E)
        q2_ref[0, pl.ds(r0, R), :] = jnp.where(lane < 64, q, 0.0).astype(BF16)
        q2_ref[1, pl.ds(r0, R), :] = jnp.where(lane >= 64, q, 0.0).astype(BF16)
        kr_ref[pl.ds(r0, R), :] = _rope64(k_ref[pl.ds(r0, R), :].astype(F32), cos, sin, lane).astype(BF16)
        vt_ref[:, pl.ds(r0, R)] = v_ref[pl.ds(r0, R), :].astype(F32).T.astype(BF16)
        return 0

    lax.fori_loop(0, S // R, prep, 0)

    lp = lam_ref[...]
    lam = (jnp.exp(jnp.sum(lp[0:1] * lp[1:2], axis=-1, keepdims=True))
           - jnp.exp(jnp.sum(lp[2:3] * lp[3:4], axis=-1, keepdims=True)) + lam_init)
    gcol = g_ref[...] * (1.0 - lam_init)

    for qi in range(S // TQ):
        kmax = (qi + 1) * TQ
        outs = []
        for mp in range(2):
            m = _scores_t(kr_ref, q2_ref[mp, qi * TQ:(qi + 1) * TQ, :], qi, s_ref.at[mp])
            l = _probs_t(s_ref.at[mp], p_ref.at[mp], m, qi + 1)
            outs.append(_dot(vt_ref[:, 0:kmax], p_ref[mp, 0:kmax, :]) / l)
        o = outs[0] - lam * outs[1]
        ms = jnp.mean(o * o, axis=0, keepdims=True)
        o = o * lax.rsqrt(ms + NORM_EPS) * gcol
        o_ref[qi * TQ:(qi + 1) * TQ, :] = o.T.astype(BF16)


def mixer_a(pn, cos_a, sin_a, lam_params, subln_g_col, *, lam_init=0.2):
    B = pn.shape[0] // SEQ
    S = SEQ
    col = lambda off: (lambda b, h: (b, off // LANE + h))
    blk = lambda off: pl.BlockSpec((S, LANE), col(off))
    tab = pl.BlockSpec((S, LANE), lambda b, h: (b, 0))
    return pl.pallas_call(
        functools.partial(_mixa_kernel, lam_init=lam_init),
        grid=(B, H_A),
        in_specs=[blk(PN_A), blk(PN_A + 512), blk(PN_A + 1024), tab, tab,
                  pl.BlockSpec((4, DH_A), lambda b, h: (0, 0)),
                  pl.BlockSpec((LANE, 1), lambda b, h: (0, 0))],
        out_specs=pl.BlockSpec((S, LANE), lambda b, h: (b, h)),
        out_shape=jax.ShapeDtypeStruct((B * S, BRANCH_W), BF16),
        scratch_shapes=[pltpu.VMEM((2, S, LANE), BF16), pltpu.VMEM((S, LANE), BF16), pltpu.VMEM((LANE, S), BF16),
                        pltpu.VMEM((2, S, TQ), F32), pltpu.VMEM((2, S, TQ), BF16)],
        compiler_params=_cparams(2),
        name="mixer_a",
    )(pn, pn, pn, cos_a, sin_a, lam_params, subln_g_col)


def _rmsnorm_rows(x, g):
    return x * lax.rsqrt(jnp.mean(x * x, axis=-1, keepdims=True) + NORM_EPS) * g


def _rope_c(x, cos, sin, lane):
    sw = jnp.where(lane < 112, pltpu.roll(x, LANE - 16, 1), pltpu.roll(x, 16, 1))
    return x * cos + sw * sin


def _mixc_kernel(cq_ref, ckv_ref, misc_ref, cos_ref, sin_ref, qg_ref, kvg_ref, wuq_ref, wuk_ref, wuvt_ref,
                 o_ref, q_scr, k_scr, vt_scr, s_ref, p_ref):
    S = SEQ
    R = 256
    scale = (NOPE_C + ROPE_C) ** -0.5 * LOG2E

    def prep(i, _):
        r0 = pl.multiple_of(i * R, R)
        lane = lax.broadcasted_iota(jnp.int32, (R, LANE), 1)
        cos = cos_ref[pl.ds(r0, R), :]
        sin = sin_ref[pl.ds(r0, R), :]
        cqn = _rmsnorm_rows(cq_ref[pl.ds(r0, R), :].astype(F32), qg_ref[...]).astype(BF16)
        q = _dot(cqn, wuq_ref[...])
        kvn = _rmsnorm_rows(ckv_ref[pl.ds(r0, R), :].astype(F32), kvg_ref[...]).astype(BF16)
        kk = _dot(kvn, wuk_ref[...])
        kpe = jnp.where(lane >= MISC_KPE, misc_ref[pl.ds(r0, R), :].astype(F32), 0.0)
        kpe = _rope_c(kpe, cos, sin, lane)
        for e in range(2):
            qe = _rope_c(q[:, e * LANE:(e + 1) * LANE], cos, sin, lane) * scale
            q_scr[e, pl.ds(r0, R), :] = qe.astype(BF16)
            k_scr[e, pl.ds(r0, R), :] = (kk[:, e * LANE:(e + 1) * LANE] + kpe).astype(BF16)
        vt_scr[:, pl.ds(r0, R)] = _nt(wuvt_ref[...], kvn).astype(BF16)
        return 0

    lax.fori_loop(0, S // R, prep, 0)

    for qi in range(S // TQ):
        kmax = (qi + 1) * TQ
        outs = []
        for e in range(2):
            m = _scores_t(k_scr.at[e], q_scr[e, qi * TQ:(qi + 1) * TQ, :], qi, s_ref.at[e])
            l = _probs_t(s_ref.at[e], p_ref.at[e], m, qi + 1)
            outs.append(_dot(vt_scr[e * V_C:(e + 1) * V_C, 0:kmax], p_ref[e, 0:kmax, :]) / l)
        o_ref[qi * TQ:(qi + 1) * TQ, :] = jnp.concatenate(outs, axis=0).T.astype(BF16)


def mixer_c(pn, cos_c, sin_c, q_norm, kv_norm, wuq, wuk, wuvt):
    B = pn.shape[0] // SEQ
    S = SEQ
    tab = pl.BlockSpec((S, LANE), lambda b, p: (b, 0))
    return pl.pallas_call(
        _mixc_kernel,
        grid=(B, H_C // 2),
        in_specs=[pl.BlockSpec((S, Q_LORA_C), lambda b, p: (b, PN_CQ // Q_LORA_C)),
                  pl.BlockSpec((S, LANE), lambda b, p: (b, PN_CKV // LANE)),
                  pl.BlockSpec((S, LANE), lambda b, p: (b, PN_MISC // LANE)),
                  tab, tab,
                  pl.BlockSpec((1, Q_LORA_C), lambda b, p: (0, 0)),
                  pl.BlockSpec((1, KV_LORA_C), lambda b, p: (0, 0)),
                  pl.BlockSpec((Q_LORA_C, 2 * LANE), lambda b, p: (0, p)),
                  pl.BlockSpec((KV_LORA_C, 2 * LANE), lambda b, p: (0, p)),
                  pl.BlockSpec((LANE, KV_LORA_C), lambda b, p: (p, 0))],
        out_specs=pl.BlockSpec((S, LANE), lambda b, p: (b, p)),
        out_shape=jax.ShapeDtypeStruct((B * S, BRANCH_W), BF16),
        scratch_shapes=[pltpu.VMEM((2, S, LANE), BF16), pltpu.VMEM((2, S, LANE), BF16), pltpu.VMEM((LANE, S), BF16),
                        pltpu.VMEM((2, S, TQ), F32), pltpu.VMEM((2, S, TQ), BF16)],
        compiler_params=_cparams(2),
        name="mixer_c",
    )(pn, pn, pn, cos_c, sin_c, q_norm, kv_norm, wuq, wuk, wuvt)


DB = 128
LSE_W = 16


def _mixd_kernel(*refs, nblk, combine):
    if combine:
        q_ref, k_ref, v_ref, cos_ref, sin_ref, o2_ref, o3_ref, l2_ref, l3_ref, o_ref, q2_scr, kr_scr = refs
    else:
        q_ref, k_ref, v_ref, cos_ref, sin_ref, o_ref, lse_ref, q2_scr, kr_scr = refs
    lane = lax.broadcasted_iota(jnp.int32, (DB, LANE), 1)

    def prep(i, _):
        r0 = pl.multiple_of(i * DB, DB)
        cos = cos_ref[0, pl.ds(r0, DB), :]
        sin = sin_ref[0, pl.ds(r0, DB), :]
        for p in range(4):
            cs = slice(p * LANE, (p + 1) * LANE)
            q = _rope64(q_ref[0, pl.ds(r0, DB), cs].astype(F32), cos, sin, lane) * (DH_D ** -0.5)
            q2_scr[0, pl.ds(r0, DB), cs] = jnp.where(lane < 64, q, 0.0).astype(BF16)
            q2_scr[1, pl.ds(r0, DB), cs] = jnp.where(lane >= 64, q, 0.0).astype(BF16)
            kr_scr[pl.ds(r0, DB), cs] = _rope64(k_ref[0, pl.ds(r0, DB), cs].astype(F32), cos, sin, lane).astype(BF16)
        return 0

    lax.fori_loop(0, nblk, prep, 0)

    def attend(j, first):
        r0 = pl.multiple_of(j * DB, DB)
        nk = DB if first else 2 * DB
        k0 = 0 if first else pl.multiple_of(r0 - DB, DB)
        a = lax.broadcasted_iota(jnp.int32, (2 * DB, nk), 0) & (DB - 1)
        b = lax.broadcasted_iota(jnp.int32, (2 * DB, nk), 1)
        valid = (b <= a) if first else ((b >= a) & (b <= a + DB))
        lse_tile = jnp.zeros((DB, LANE), F32)
        if combine:
            l2t = l2_ref[0, pl.ds(r0, DB), :]
            l3t = l3_ref[0, pl.ds(r0, DB), :]
        for p in range(4):
            cs = slice(p * LANE, (p + 1) * LANE)
            qs = jnp.concatenate([q2_scr[0, pl.ds(r0, DB), cs], q2_scr[1, pl.ds(r0, DB), cs]], axis=0)
            s = jnp.where(valid, _nt(qs, kr_scr[pl.ds(k0, nk), cs]), -jnp.inf)
            m = jnp.max(s, axis=-1, keepdims=True)
            pr = jnp.exp(s - m)
            l = jnp.sum(pr, axis=-1, keepdims=True)
            o = _dot(pr.astype(BF16), v_ref[0, pl.ds(k0, nk), cs]) / l
            lse = m + jnp.log(l)
            o1 = jnp.where(lane < 64, o[:DB], o[DB:])
            if combine:
                c0 = 2 * p * LSE_W
                c1 = c0 + LSE_W
                L1 = jnp.where(lane < 64, lse[:DB], lse[DB:])
                L2 = jnp.where(lane < 64, l2t[:, c0:c0 + 1], l2t[:, c1:c1 + 1])
                L3 = jnp.where(lane < 64, l3t[:, c0:c0 + 1], l3t[:, c1:c1 + 1])
                mx = jnp.maximum(L1, jnp.maximum(L2, L3))
                w1 = jnp.exp(L1 - mx)
                w2 = jnp.exp(L2 - mx)
                w3 = jnp.exp(L3 - mx)
                o2 = o2_ref[0, pl.ds(r0, DB), cs].astype(F32)
                o3 = o3_ref[0, pl.ds(r0, DB), cs].astype(F32)
                o1 = (w1 * o1 + w2 * o2 + w3 * o3) / (w1 + w2 + w3)
            else:
                hd = lane // LSE_W
                lse_tile = jnp.where(hd == 2 * p, lse[:DB], jnp.where(hd == 2 * p + 1, lse[DB:], lse_tile))
            o_ref[0, pl.ds(r0, DB), cs] = o1.astype(BF16)
        if not combine:
            lse_ref[0, pl.ds(r0, DB), :] = lse_tile

    attend(0, True)
    if nblk > 1:
        def body(j, _):
            attend(j, False)
            return 0
        lax.fori_loop(1, nblk, body, 0)


def _mixd_call(qkv3, qcol0, cos3, sin3, prev, *, dil, B):
    R = SEQ // dil
    nblk = R // DB
    combine = prev is not None
    qspec = lambda c: pl.BlockSpec((1, R, BRANCH_W), lambda b, r: (b * dil + r, 0, qcol0 + c))
    nat = lambda w: pl.BlockSpec((1, R, w), lambda b, r: (b, 0, r))
    in_specs = [qspec(0), qspec(1), qspec(2), nat(LANE), nat(LANE)]
    args = [qkv3, qkv3, qkv3, cos3, sin3]
    o_sds = jax.ShapeDtypeStruct((B, R, dil * BRANCH_W), BF16)
    if combine:
        in_specs += [nat(BRANCH_W), nat(BRANCH_W), nat(LANE), nat(LANE)]
        args += list(prev)
        out_specs, out_shape = nat(BRANCH_W), o_sds
    else:
        out_specs = [nat(BRANCH_W), nat(LANE)]
        out_shape = [o_sds, jax.ShapeDtypeStruct((B, R, dil * LANE), F32)]
    return pl.pallas_call(
        functools.partial(_mixd_kernel, nblk=nblk, combine=combine),
        grid=(B, dil),
        in_specs=in_specs,
        out_specs=out_specs,
        out_shape=out_shape,
        scratch_shapes=[pltpu.VMEM((2, R, BRANCH_W), BF16), pltpu.VMEM((R, BRANCH_W), BF16)],
        compiler_params=_cparams(2),
        name=f"mixer_d_dil{dil}",
    )(*args)


def mixer_d(pn, p2, p3, cos_a, sin_a):
    B = pn.shape[0] // SEQ
    S = SEQ
    o3, l3 = _mixd_call(p3.reshape(B * 16, S // 16, 1536), 0, cos_a.reshape(B, S // 16, 16 * LANE),
                        sin_a.reshape(B, S // 16, 16 * LANE), None, dil=16, B=B)
    o2, l2 = _mixd_call(p2.reshape(B * 4, S // 4, 1536), 0, cos_a.reshape(B, S // 4, 4 * LANE),
                        sin_a.reshape(B, S // 4, 4 * LANE), None, dil=4, B=B)
    prev = (o2.reshape(B, S, BRANCH_W), o3.reshape(B, S, BRANCH_W), l2.reshape(B, S, LANE), l3.reshape(B, S, LANE))
    o = _mixd_call(pn.reshape(B, S, PN_W), PN_D1 // BRANCH_W, cos_a.reshape(B, S, LANE), sin_a.reshape(B, S, LANE),
                   prev, dil=1, B=B)
    return o.reshape(B * S, BRANCH_W)


GB = 256
NCHUNK = SEQ // CHUNK_B
PAD_B = 8


def _split3(x):
    hi = x.astype(BF16)
    r1 = x - hi.astype(F32)
    mid = r1.astype(BF16)
    lo = (r1 - mid.astype(F32)).astype(BF16)
    return hi, mid, lo


def _mixb_kernel(qkv_ref, misc_ref, conv_ref, alog_ref, dt_ref, ng_ref, o_ref,
                 xpad, c_scr, gc_scr, gl_scr, gct_scr, qeff_scr, o0_scr, n_scr, g_scr, s_scr):
    S = SEQ
    R = GB
    NG = S // R
    ri = lax.broadcasted_iota(jnp.int32, (R, R), 0)
    ci = lax.broadcasted_iota(jnp.int32, (R, R), 1)
    same = (ri // CHUNK_B) == (ci // CHUNK_B)
    tril = same & (ri >= ci)
    strict = same & (ri > ci)
    eye = jnp.where(ri == ci, 1.0, 0.0)

    sel = jnp.concatenate([jnp.where(tril, 1.0, 0.0), jnp.where(same, 1.0, 0.0)], axis=0).astype(BF16)
    neg_a = -jnp.exp(alog_ref[...])

    def gprep(i, _):
        r0 = pl.multiple_of(i * R, R)
        x = misc_ref[pl.ds(r0, R), :].astype(F32) + dt_ref[...]
        sp = jnp.maximum(x, 0.0) + jnp.log1p(jnp.exp(-jnp.abs(x)))
        g = neg_a * sp
        hi, mid, lo = _split3(g)
        acc = _dot(sel, jnp.concatenate([hi, mid, lo], axis=1))
        acc = acc[:, 0:LANE] + acc[:, LANE:2 * LANE] + acc[:, 2 * LANE:3 * LANE]
        gc_scr[pl.ds(r0, R), :] = acc[:R]
        gl_scr[pl.ds(r0, R), :] = acc[R:]
        gct_scr[i] = acc[:R].T[0:8, :]
        return 0

    lax.fori_loop(0, NG, gprep, 0)

    xpad[0:PAD_B, :] = jnp.zeros((PAD_B, LANE), F32)

    def conv_head(h):
        for t in range(3):
            cb = t * H_B + h
            cs = slice(cb * LANE, (cb + 1) * LANE)

            def fill(i, _):
                r0 = pl.multiple_of(i * R, R)
                xpad[pl.ds(PAD_B + r0, R), :] = qkv_ref[pl.ds(r0, R), cs].astype(F32)
                return 0

            lax.fori_loop(0, NG, fill, 0)

            def conv(i, _):
                r0 = pl.multiple_of(i * R, R)
                y = jnp.zeros((R, LANE), F32)
                for j in range(CONV_B):
                    y = y + xpad[pl.ds(r0 + (PAD_B - (CONV_B - 1) + j), R), :] * conv_ref[j:j + 1, cs]
                y = _silu(y)
                if t < 2:
                    y = y * lax.rsqrt(jnp.sum(y * y, axis=-1, keepdims=True) + NORM_EPS)
                c_scr[pl.ds(r0, R), (3 * h + t) * LANE:(3 * h + t + 1) * LANE] = y
                return 0

            lax.fori_loop(0, NG, conv, 0)

    def group_head(i, h):
        r0 = pl.multiple_of(i * R, R)
        qn = c_scr[pl.ds(r0, R), (3 * h) * LANE:(3 * h + 1) * LANE]
        kn = c_scr[pl.ds(r0, R), (3 * h + 1) * LANE:(3 * h + 2) * LANE]
        v = c_scr[pl.ds(r0, R), (3 * h + 2) * LANE:(3 * h + 3) * LANE]
        mf = misc_ref[pl.ds(r0, R), :].astype(F32)
        beta = _sigmoid(mf[:, MISC_BETA + h:MISC_BETA + h + 1])
        gcol = gc_scr[pl.ds(r0, R), MISC_DECAY + h:MISC_DECAY + h + 1]
        glcol = gl_scr[pl.ds(r0, R), MISC_DECAY + h:MISC_DECAY + h + 1]
        grow = gct_scr[i, MISC_DECAY + h:MISC_DECAY + h + 1, :]
        dec = jnp.exp(jnp.where(tril, gcol - grow, 0.0))
        kb = kn * beta
        knb = kn.astype(BF16)
        lm = jnp.where(strict, _nt(kb.astype(BF16), knb) * dec, 0.0)
        qkm = jnp.where(tril, _nt((qn * (DK_B ** -0.5)).astype(BF16), knb) * dec, 0.0)
        yield
        x = eye - jnp.where((ri // 2) == (ci // 2), lm, 0.0)
        s = 2
        while s < CHUNK_B:
            off = ((ri // (2 * s)) == (ci // (2 * s))) & ((ri // s) != (ci // s))
            xb = x.astype(BF16)
            t = _dot(jnp.where(off, lm, 0.0).astype(BF16), xb).astype(BF16)
            yield
            x = x - _dot(xb, t)
            yield
            s *= 2
        rhs = jnp.concatenate([v * beta, kb * jnp.exp(gcol)], axis=1).astype(BF16)
        uw = _dot(x.astype(BF16), rhs)
        yield
        uwb = uw.astype(BF16)
        ox = _dot(qkm.astype(BF16), uwb)
        yield
        qeff_scr[h, pl.ds(r0, R), :] = (qn * (DK_B ** -0.5) * jnp.exp(gcol) - ox[:, LANE:]).astype(BF16)
        o0_scr[h, pl.ds(r0, R), :] = ox[:, :LANE]
        kt = (kn * jnp.exp(glcol - gcol)).astype(BF16)
        for c in range(R // CHUNK_B):
            gn = _tn(kt[c * CHUNK_B:(c + 1) * CHUNK_B], uwb[c * CHUNK_B:(c + 1) * CHUNK_B])
            n_scr[h, i * (R // CHUNK_B) + c] = gn[:, :LANE]
            g_scr[h, i * (R // CHUNK_B) + c] = gn[:, LANE:].astype(BF16)

    for h in range(H_B):
        conv_head(h)

    def group(i, _):
        gens = [group_head(i, h) for h in range(H_B)]
        while gens:
            gens = [g for g in gens if next(g, True) is None]
        return 0

    lax.fori_loop(0, NG, group, 0)

    s_scr[...] = jnp.zeros(s_scr.shape, F32)

    def scan(c, _):
        r0 = pl.multiple_of(c * CHUNK_B, CHUNK_B)
        glrow = jnp.exp(gl_scr[pl.ds(r0, 1), :])
        for h in range(H_B):
            st = s_scr[h]
            sb = st.astype(BF16)
            o = _dot(qeff_scr[h, pl.ds(r0, CHUNK_B), :], sb) + o0_scr[h, pl.ds(r0, CHUNK_B), :]
            o = _rmsnorm_rows(o, ng_ref[...])
            o_ref[pl.ds(r0, CHUNK_B), h * LANE:(h + 1) * LANE] = o.astype(BF16)
            a = glrow[:, MISC_DECAY + h:MISC_DECAY + h + 1]
            s_scr[h] = a * st - _dot(g_scr[h, c], sb) + n_scr[h, c]
        return 0

    lax.fori_loop(0, NCHUNK, scan, 0)


def mixer_b(pn, conv_w, alog_row, dt_row, out_norm):
    B = pn.shape[0] // SEQ
    S = SEQ
    W = 3 * H_B * DK_B
    row = pl.BlockSpec((1, LANE), lambda b: (0, 0))
    return pl.pallas_call(
        _mixb_kernel,
        grid=(B,),
        in_specs=[pl.BlockSpec((S, W), lambda b: (b, PN_B // W), pipeline_mode=pl.Buffered(1)),
                  pl.BlockSpec((S, LANE), lambda b: (b, PN_MISC // LANE)),
                  pl.BlockSpec((CONV_B, W), lambda b: (0, 0)), row, row, row],
        out_specs=pl.BlockSpec((S, BRANCH_W), lambda b: (b, 0)),
        out_shape=jax.ShapeDtypeStruct((B * S, BRANCH_W), BF16),
        scratch_shapes=[pltpu.VMEM((PAD_B + S, LANE), F32),
                        pltpu.VMEM((S, 3 * H_B * LANE), F32),
                        pltpu.VMEM((S, LANE), F32),
                        pltpu.VMEM((S, LANE), F32),
                        pltpu.VMEM((S // GB, 8, GB), F32),
                        pltpu.VMEM((H_B, S, LANE), BF16),
                        pltpu.VMEM((H_B, S, LANE), F32),
                        pltpu.VMEM((H_B, NCHUNK, DK_B, DV_B), F32),
                        pltpu.VMEM((H_B, NCHUNK, DK_B, DK_B), BF16),
                        pltpu.VMEM((H_B, DK_B, DV_B), F32)],
        compiler_params=_cparams(1),
        name="mixer_b",
    )(pn, pn, conv_w, alog_row, dt_row, out_norm)


TM_MERGE = 512


def _merge_kernel(*refs):
    o_refs = refs[0:4]
    z_refs = refs[4:8]
    m_refs = refs[8:12]
    x_ref, mod_ref, wbr_ref, wout_ref, lng_ref, lnb_ref, out_ref = refs[12:]
    merged = jnp.zeros((TM_MERGE, D_MODEL), F32)
    for n in range(N_BRANCH):
        br = (o_refs[n][...].astype(F32) * _silu(z_refs[n][...].astype(F32))).astype(BF16)
        merged = merged + _dot(br, wbr_ref[n]) * _sigmoid(m_refs[n][...].astype(F32))
    y = _dot(merged.astype(BF16), wout_ref[...])
    t = DEEPNORM_ALPHA * x_ref[...] + mod_ref[0, 2:3, :] * y
    mu = jnp.mean(t, axis=-1, keepdims=True)
    tc = t - mu
    var = jnp.mean(tc * tc, axis=-1, keepdims=True)
    out_ref[...] = tc * lax.rsqrt(var + NORM_EPS) * lng_ref[...] + lnb_ref[...]


def merge_out(pn, o_a, o_b, o_c, o_d, x2, mod, w_br, w_out, ln_g, ln_b):
    T = x2.shape[0]
    tm = TM_MERGE
    per_b = SEQ // tm
    br = pl.BlockSpec((tm, BRANCH_W), lambda i: (i, 0))
    zspec = lambda n: pl.BlockSpec((tm, BRANCH_W), lambda i: (i, PN_Z // BRANCH_W + n))
    mspec = lambda n: pl.BlockSpec((tm, D_MODEL), lambda i: (i, PN_MERGE // D_MODEL + n))
    row = pl.BlockSpec((1, D_MODEL), lambda i: (0, 0))
    return pl.pallas_call(
        _merge_kernel,
        grid=(T // tm,),
        in_specs=[br, br, br, br] + [zspec(n) for n in range(4)] + [mspec(n) for n in range(4)] + [
            pl.BlockSpec((tm, D_MODEL), lambda i: (i, 0)),
            pl.BlockSpec((1, 3, D_MODEL), lambda i: (i // per_b, 0, 0)),
            pl.BlockSpec((N_BRANCH, BRANCH_W, D_MODEL), lambda i: (0, 0, 0)),
            pl.BlockSpec((D_MODEL, D_MODEL), lambda i: (0, 0)), row, row],
        out_specs=pl.BlockSpec((tm, D_MODEL), lambda i: (i, 0)),
        out_shape=jax.ShapeDtypeStruct((T, D_MODEL), F32),
        compiler_params=_cparams(1),
        name="merge_out",
    )(o_a, o_b, o_c, o_d, pn, pn, pn, pn, pn, pn, pn, pn, x2, mod, w_br, w_out, ln_g, ln_b)


def _prep_w_in(w):
    d = lambda part, g: w[:, _OFF_D + part * 1536 + g * 512:_OFF_D + part * 1536 + (g + 1) * 512]
    zc = lambda n: jnp.zeros((D_MODEL, n), w.dtype)
    misc = jnp.concatenate([w[:, _OFF_BETA:_OFF_BETA + 4], w[:, _OFF_DECAY:_OFF_DECAY + 4], zc(MISC_KPE - 8),
                            w[:, _OFF_KPE:_OFF_KPE + ROPE_C]], axis=1)
    wn = jnp.concatenate([w[:, _OFF_A:_OFF_A + 1536], w[:, _OFF_BQKV:_OFF_BQKV + 1536],
                          w[:, _OFF_CQ:_OFF_CQ + Q_LORA_C], w[:, _OFF_CKV:_OFF_CKV + KV_LORA_C], misc,
                          d(0, 0), d(1, 0), d(2, 0), w[:, _OFF_Z:_OFF_Z + 2048], w[:, _OFF_MERGE:_OFF_MERGE + 4096]],
                         axis=1)
    w2 = jnp.concatenate([d(0, 1), d(1, 1), d(2, 1)], axis=1)
    w3 = jnp.concatenate([d(0, 2), d(1, 2), d(2, 2)], axis=1)
    return wn.astype(BF16), w2.astype(BF16), w3.astype(BF16)


def _prep_w_c(w_uq, w_ukv):
    q = w_uq.reshape(Q_LORA_C, H_C, NOPE_C + ROPE_C)
    wq = jnp.concatenate([q[..., :NOPE_C], jnp.zeros((Q_LORA_C, H_C, LANE - NOPE_C - ROPE_C), q.dtype),
                          q[..., NOPE_C:]], axis=-1).reshape(Q_LORA_C, H_C * LANE)
    kv = w_ukv.reshape(KV_LORA_C, H_C, NOPE_C + V_C)
    wk = jnp.concatenate([kv[..., :NOPE_C], jnp.zeros((KV_LORA_C, H_C, LANE - NOPE_C), kv.dtype)],
                         axis=-1).reshape(KV_LORA_C, H_C * LANE)
    wvt = kv[..., NOPE_C:].reshape(KV_LORA_C, H_C * V_C).T
    return wq.astype(BF16), wk.astype(BF16), wvt.astype(BF16)


def _lane_row(vals, at):
    return jnp.zeros((1, LANE), F32).at[0, at:at + vals.shape[0]].set(vals.astype(F32))


def kernel(x, c, positions, w_ada, b_ada, w_in, conv_b, a_log, dt_bias, out_norm_b, lambda_q1, lambda_k1,
           lambda_q2, lambda_k2, subln_g, q_norm_c, w_uq, kv_norm_c, w_ukv, w_br, w_out, ln_g, ln_b):
    B, S, D = x.shape
    assert (S, D) == (SEQ, D_MODEL) and w_in.shape[0] == DEPTH
    T = B * S
    mods = ada_modulation(c, w_ada, b_ada)
    cos_a, sin_a, cos_c, sin_c = rope_tables(positions)
    x2 = x.reshape(T, D)
    for l in range(DEPTH):
        mod = mods[l]
        wn, w2, w3 = _prep_w_in(w_in[l])
        pn = input_projection(x2.reshape(T // 1024, 1024, D), mod, wn, nslab=1, rows=1024, steps_per_batch=S // 1024,
                              tn=1024)
        p2 = input_projection(x2.reshape(B, S // 4, 4 * D), mod, w2, nslab=1, rows=S // 4, steps_per_batch=4, tn=1536)
        p3 = input_projection(x2.reshape(B, S // 16, 16 * D), mod, w3, nslab=4, rows=S // 16, steps_per_batch=4,
                              tn=1536)
        lam = jnp.stack([lambda_q1[l], lambda_k1[l], lambda_q2[l], lambda_k2[l]]).astype(F32)
        o_a = mixer_a(pn, cos_a, sin_a, lam, subln_g[l].reshape(LANE, 1),
                      lam_init=0.8 - 0.6 * math.exp(-0.3 * l))
        o_b = mixer_b(pn, conv_b[l], _lane_row(a_log[l], MISC_DECAY), _lane_row(dt_bias[l], MISC_DECAY),
                      out_norm_b[l].reshape(1, LANE))
        wq, wk, wv = _prep_w_c(w_uq[l], w_ukv[l])
        o_c = mixer_c(pn, cos_c, sin_c, q_norm_c[l].reshape(1, Q_LORA_C), kv_norm_c[l].reshape(1, KV_LORA_C),
                      wq, wk, wv)
        o_d = mixer_d(pn, p2, p3, cos_a, sin_a)
        x2 = merge_out(pn, o_a, o_b, o_c, o_d, x2, mod, w_br[l].astype(BF16), w_out[l].astype(BF16),
                       ln_g[l].reshape(1, D), ln_b[l].reshape(1, D))
    return x2.reshape(B, S, D)
```

```python
import functools
import math

import jax
import jax.numpy as jnp
from jax import lax
from jax.experimental import pallas as pl
from jax.experimental.pallas import tpu as pltpu

F32 = jnp.float32
BF16 = jnp.bfloat16

D_MODEL = 1024
SEQ = 2048
DEPTH = 2
ROPE_THETA = 500000.0
NORM_EPS = 1e-6
H_A, DH_A = 4, 64
H_B, DK_B, DV_B, CONV_B, CHUNK_B = 4, 128, 128, 4, 64
H_C, Q_LORA_C, KV_LORA_C, NOPE_C, ROPE_C, V_C = 8, 256, 128, 64, 32, 64
H_D, DH_D = 8, 64
DIL_GROUPS = ((128, 1), (512, 4), (2048, 16))
N_BRANCH, BRANCH_W = 4, 512
DEEPNORM_ALPHA = (2.0 * DEPTH) ** 0.25

LANE = 128
VMEM_LIMIT = 56 * 1024 * 1024

_OFF_A = 0
_OFF_BQKV = 1536
_OFF_BETA = 3072
_OFF_DECAY = 3076
_OFF_CQ = 3080
_OFF_CKV = 3336
_OFF_KPE = 3464
_OFF_D = 3496
_OFF_Z = 8104
_OFF_MERGE = 10152
D_IN = 14248

PN_A = 0
PN_B = 1536
PN_CQ = 3072
PN_CKV = 3328
PN_MISC = 3456
PN_D = 3584
PN_Z = 8192
PN_MERGE = 10240
PN_W = 14336
MISC_BETA, MISC_DECAY, MISC_KPE = 0, 4, 96


def _cparams(n_grid):
    return pltpu.CompilerParams(dimension_semantics=("arbitrary",) * n_grid,
                                vmem_limit_bytes=VMEM_LIMIT)


def _nt(a, b):
    return lax.dot_general(a, b, (((1,), (1,)), ((), ())), preferred_element_type=F32)


def _tn(a, b):
    return lax.dot_general(a, b, (((0,), (0,)), ((), ())), preferred_element_type=F32)


def _dot(a, b):
    return jnp.dot(a, b, preferred_element_type=F32)


def _sigmoid(x):
    return 1.0 / (1.0 + jnp.exp(-x))


def _silu(x):
    return x * _sigmoid(x)


def _ada_kernel(c_ref, w_ref, b_ref, o_ref):
    ca = _silu(c_ref[...]).astype(BF16)
    o_ref[0] = _dot(ca, w_ref[0].astype(BF16)) + b_ref[0]


def ada_modulation(c, w_ada, b_ada):
    B = c.shape[0]
    L = w_ada.shape[0]
    out = pl.pallas_call(
        _ada_kernel,
        grid=(L, 3),
        in_specs=[pl.BlockSpec((B, D_MODEL), lambda l, j: (0, 0)),
                  pl.BlockSpec((1, D_MODEL, D_MODEL), lambda l, j: (l, 0, j)),
                  pl.BlockSpec((1, 1, D_MODEL), lambda l, j: (l, 0, j))],
        out_specs=pl.BlockSpec((1, B, D_MODEL), lambda l, j: (l, 0, j)),
        out_shape=jax.ShapeDtypeStruct((L, B, 3 * D_MODEL), F32),
        compiler_params=_cparams(2),
        name="ada_modulation",
    )(c, w_ada, b_ada.reshape(L, 1, 3 * D_MODEL))
    return out.reshape(L, B, 3, D_MODEL)


def _rope_rows():
    half_a = (DH_A // 4) // 2
    inv_a = ROPE_THETA ** (-jnp.arange(half_a, dtype=F32) / half_a)
    half_c = ROPE_C // 2
    inv_c = ROPE_THETA ** (-jnp.arange(half_c, dtype=F32) / half_c)
    z = jnp.zeros
    inv = jnp.concatenate([inv_a, inv_a, z(48, F32), inv_a, inv_a, z(16, F32), inv_c, inv_c])
    sgn = jnp.concatenate([-jnp.ones(8, F32), jnp.ones(8, F32), z(48, F32),
                           -jnp.ones(8, F32), jnp.ones(8, F32), z(16, F32),
                           -jnp.ones(16, F32), jnp.ones(16, F32)])
    return inv.reshape(1, LANE), sgn.reshape(1, LANE)


def _rope_table_kernel(pos_ref, inv_ref, sgn_ref, cos_a, sin_a, cos_c, sin_c):
    R = 256

    def body(i, _):
        r0 = pl.multiple_of(i * R, R)
        ang = pos_ref[pl.ds(r0, R), :].astype(F32) * inv_ref[...]
        cv = jnp.cos(ang)
        sv = jnp.sin(ang) * sgn_ref[...]
        lane = lax.broadcasted_iota(jnp.int32, (R, LANE), 1)
        is_c = lane >= MISC_KPE
        cos_a[pl.ds(r0, R), :] = jnp.where(is_c, 1.0, cv)
        sin_a[pl.ds(r0, R), :] = jnp.where(is_c, 0.0, sv)
        cos_c[pl.ds(r0, R), :] = jnp.where(is_c, cv, 1.0)
        sin_c[pl.ds(r0, R), :] = jnp.where(is_c, sv, 0.0)
        return 0

    lax.fori_loop(0, SEQ // R, body, 0)


def rope_tables(positions):
    B, S = positions.shape
    inv, sgn = _rope_rows()
    tab = jax.ShapeDtypeStruct((B * S, LANE), F32)
    row = pl.BlockSpec((1, LANE), lambda b: (0, 0))
    blk = pl.BlockSpec((S, LANE), lambda b: (b, 0))
    return pl.pallas_call(
        _rope_table_kernel,
        grid=(B,),
        in_specs=[pl.BlockSpec((S, 1), lambda b: (b, 0)), row, row],
        out_specs=[blk, blk, blk, blk],
        out_shape=[tab, tab, tab, tab],
        compiler_params=_cparams(1),
        name="rope_tables",
    )(positions.reshape(B * S, 1), inv, sgn)


TM_IN = 1024
TN_IN = 2048


def _inproj_kernel(x_ref, mod_ref, w_ref, o_ref, h_ref):
    @pl.when(pl.program_id(1) == 0)
    def _():
        shift = mod_ref[0, 0:1, :]
        scale1 = 1.0 + mod_ref[0, 1:2, :]
        R = 128

        def body(i, _):
            r0 = pl.multiple_of(i * R, R)
            xs = x_ref[pl.ds(r0, R), :]
            mu = jnp.mean(xs, axis=-1, keepdims=True)
            xc = xs - mu
            var = jnp.mean(xc * xc, axis=-1, keepdims=True)
            hn = xc * lax.rsqrt(var + NORM_EPS)
            h_ref[pl.ds(r0, R), :] = (hn * scale1 + shift).astype(BF16)
            return 0

        lax.fori_loop(0, TM_IN // R, body, 0)

    o_ref[...] = _dot(h_ref[...], w_ref[...]).astype(BF16)


def input_projection(x2, mod, w):
    T = x2.shape[0]
    ncol = w.shape[1]
    return pl.pallas_call(
        _inproj_kernel,
        grid=(T // TM_IN, ncol // TN_IN),
        in_specs=[pl.BlockSpec((TM_IN, D_MODEL), lambda i, j: (i, 0)),
                  pl.BlockSpec((1, 3, D_MODEL), lambda i, j: (i // (SEQ // TM_IN), 0, 0)),
                  pl.BlockSpec((D_MODEL, TN_IN), lambda i, j: (0, j))],
        out_specs=pl.BlockSpec((TM_IN, TN_IN), lambda i, j: (i, j)),
        out_shape=jax.ShapeDtypeStruct((T, ncol), BF16),
        scratch_shapes=[pltpu.VMEM((TM_IN, D_MODEL), BF16)],
        compiler_params=_cparams(2),
        name="input_projection",
    )(x2, mod, w)


def _rope64(x, cos, sin, lane):
    sw = jnp.where((lane & 63) < 8, pltpu.roll(x, LANE - 8, 1), pltpu.roll(x, 8, 1))
    return x * cos + sw * sin


TQ = 256
TKB = 256
LOG2E = 1.4426950408889634


def _scores_t(k_ref, q_blk, qi, s_ref, chunk=512):
    kmax = (qi + 1) * TQ
    m = None
    for c0 in range(0, kmax, chunk):
        c1 = min(c0 + chunk, kmax)
        s = _nt(k_ref[c0:c1, :], q_blk)
        if c1 == kmax:
            kk = lax.broadcasted_iota(jnp.int32, (c1 - c0, TQ), 0) + (c0 - qi * TQ)
            qq = lax.broadcasted_iota(jnp.int32, (c1 - c0, TQ), 1)
            s = jnp.where(kk <= qq, s, -jnp.inf)
        s_ref[c0:c1, :] = s
        mc = jnp.max(s, axis=0, keepdims=True)
        m = mc if m is None else jnp.maximum(m, mc)
    return m


def _probs_t(s_ref, p_ref, m, ntiles):
    def body(t, l):
        r0 = pl.multiple_of(t * TKB, TKB)
        p = jnp.exp2(s_ref[pl.ds(r0, TKB), :] - m)
        p_ref[pl.ds(r0, TKB), :] = p.astype(BF16)
        return l + jnp.sum(p, axis=0, keepdims=True)
    return lax.fori_loop(0, ntiles, body, jnp.zeros((1, TQ), F32))


def _mixa_kernel(q_ref, k_ref, v_ref, cos_ref, sin_ref, lam_ref, g_ref, o_ref, q2_ref, kr_ref, vt_ref, s_ref, p_ref,
                 *, lam_init):
    S = SEQ
    R = 128

    def prep(i, _):
        r0 = pl.multiple_of(i * R, R)
        lane = lax.broadcasted_iota(jnp.int32, (R, LANE), 1)
        cos = cos_ref[pl.ds(r0, R), :]
        sin = sin_ref[pl.ds(r0, R), :]
        q = _rope64(q_ref[pl.ds(r0, R), :].astype(F32), cos, sin, lane) * (DH_A ** -0.5 * LOG2E)
        q2_ref[0, pl.ds(r0, R), :] = jnp.where(lane < 64, q, 0.0).astype(BF16)
        q2_ref[1, pl.ds(r0, R), :] = jnp.where(lane >= 64, q, 0.0).astype(BF16)
        kr_ref[pl.ds(r0, R), :] = _rope64(k_ref[pl.ds(r0, R), :].astype(F32), cos, sin, lane).astype(BF16)
        vt_ref[:, pl.ds(r0, R)] = v_ref[pl.ds(r0, R), :].astype(F32).T.astype(BF16)
        return 0

    lax.fori_loop(0, S // R, prep, 0)

    lp = lam_ref[...]
    lam = (jnp.exp(jnp.sum(lp[0:1] * lp[1:2], axis=-1, keepdims=True))
           - jnp.exp(jnp.sum(lp[2:3] * lp[3:4], axis=-1, keepdims=True)) + lam_init)
    gcol = g_ref[...] * (1.0 - lam_init)

    for qi in range(S // TQ):
        kmax = (qi + 1) * TQ
        outs = []
        for mp in range(2):
            m = _scores_t(kr_ref, q2_ref[mp, qi * TQ:(qi + 1) * TQ, :], qi, s_ref.at[mp])
            l = _probs_t(s_ref.at[mp], p_ref.at[mp], m, qi + 1)
            outs.append(_dot(vt_ref[:, 0:kmax], p_ref[mp, 0:kmax, :]) / l)
        o = outs[0] - lam * outs[1]
        ms = jnp.mean(o * o, axis=0, keepdims=True)
        o = o * lax.rsqrt(ms + NORM_EPS) * gcol
        o_ref[qi * TQ:(qi + 1) * TQ, :] = o.T.astype(BF16)


def mixer_a(pn, cos_a, sin_a, lam_params, subln_g_col, *, lam_init=0.2):
    B = pn.shape[0] // SEQ
    S = SEQ
    col = lambda off: (lambda b, h: (b, off // LANE + h))
    blk = lambda off: pl.BlockSpec((S, LANE), col(off))
    tab = pl.BlockSpec((S, LANE), lambda b, h: (b, 0))
    return pl.pallas_call(
        functools.partial(_mixa_kernel, lam_init=lam_init),
        grid=(B, H_A),
        in_specs=[blk(PN_A), blk(PN_A + 512), blk(PN_A + 1024), tab, tab,
                  pl.BlockSpec((4, DH_A), lambda b, h: (0, 0)),
                  pl.BlockSpec((LANE, 1), lambda b, h: (0, 0))],
        out_specs=pl.BlockSpec((S, LANE), lambda b, h: (b, h)),
        out_shape=jax.ShapeDtypeStruct((B * S, BRANCH_W), BF16),
        scratch_shapes=[pltpu.VMEM((2, S, LANE), BF16), pltpu.VMEM((S, LANE), BF16), pltpu.VMEM((LANE, S), BF16),
                        pltpu.VMEM((2, S, TQ), F32), pltpu.VMEM((2, S, TQ), BF16)],
        compiler_params=_cparams(2),
        name="mixer_a",
    )(pn, pn, pn, cos_a, sin_a, lam_params, subln_g_col)


def _rmsnorm_rows(x, g):
    return x * lax.rsqrt(jnp.mean(x * x, axis=-1, keepdims=True) + NORM_EPS) * g


def _rope_c(x, cos, sin, lane):
    sw = jnp.where(lane < 112, pltpu.roll(x, LANE - 16, 1), pltpu.roll(x, 16, 1))
    return x * cos + sw * sin


def _mixc_kernel(cq_ref, ckv_ref, misc_ref, cos_ref, sin_ref, qg_ref, kvg_ref, wuq_ref, wuk_ref, wuvt_ref,
                 o_ref, q_scr, k_scr, vt_scr, s_ref, p_ref):
    S = SEQ
    R = 256
    scale = (NOPE_C + ROPE_C) ** -0.5 * LOG2E

    def prep(i, _):
        r0 = pl.multiple_of(i * R, R)
        lane = lax.broadcasted_iota(jnp.int32, (R, LANE), 1)
        cos = cos_ref[pl.ds(r0, R), :]
        sin = sin_ref[pl.ds(r0, R), :]
        cqn = _rmsnorm_rows(cq_ref[pl.ds(r0, R), :].astype(F32), qg_ref[...]).astype(BF16)
        q = _dot(cqn, wuq_ref[...])
        kvn = _rmsnorm_rows(ckv_ref[pl.ds(r0, R), :].astype(F32), kvg_ref[...]).astype(BF16)
        kk = _dot(kvn, wuk_ref[...])
        kpe = jnp.where(lane >= MISC_KPE, misc_ref[pl.ds(r0, R), :].astype(F32), 0.0)
        kpe = _rope_c(kpe, cos, sin, lane)
        for e in range(2):
            qe = _rope_c(q[:, e * LANE:(e + 1) * LANE], cos, sin, lane) * scale
            q_scr[e, pl.ds(r0, R), :] = qe.astype(BF16)
            k_scr[e, pl.ds(r0, R), :] = (kk[:, e * LANE:(e + 1) * LANE] + kpe).astype(BF16)
        vt_scr[:, pl.ds(r0, R)] = _nt(wuvt_ref[...], kvn).astype(BF16)
        return 0

    lax.fori_loop(0, S // R, prep, 0)

    for qi in range(S // TQ):
        kmax = (qi + 1) * TQ
        outs = []
        for e in range(2):
            m = _scores_t(k_scr.at[e], q_scr[e, qi * TQ:(qi + 1) * TQ, :], qi, s_ref.at[e])
            l = _probs_t(s_ref.at[e], p_ref.at[e], m, qi + 1)
            outs.append(_dot(vt_scr[e * V_C:(e + 1) * V_C, 0:kmax], p_ref[e, 0:kmax, :]) / l)
        o_ref[qi * TQ:(qi + 1) * TQ, :] = jnp.concatenate(outs, axis=0).T.astype(BF16)


def mixer_c(pn, cos_c, sin_c, q_norm, kv_norm, wuq, wuk, wuvt):
    B = pn.shape[0] // SEQ
    S = SEQ
    tab = pl.BlockSpec((S, LANE), lambda b, p: (b, 0))
    return pl.pallas_call(
        _mixc_kernel,
        grid=(B, H_C // 2),
        in_specs=[pl.BlockSpec((S, Q_LORA_C), lambda b, p: (b, PN_CQ // Q_LORA_C)),
                  pl.BlockSpec((S, LANE), lambda b, p: (b, PN_CKV // LANE)),
                  pl.BlockSpec((S, LANE), lambda b, p: (b, PN_MISC // LANE)),
                  tab, tab,
                  pl.BlockSpec((1, Q_LORA_C), lambda b, p: (0, 0)),
                  pl.BlockSpec((1, KV_LORA_C), lambda b, p: (0, 0)),
                  pl.BlockSpec((Q_LORA_C, 2 * LANE), lambda b, p: (0, p)),
                  pl.BlockSpec((KV_LORA_C, 2 * LANE), lambda b, p: (0, p)),
                  pl.BlockSpec((LANE, KV_LORA_C), lambda b, p: (p, 0))],
        out_specs=pl.BlockSpec((S, LANE), lambda b, p: (b, p)),
        out_shape=jax.ShapeDtypeStruct((B * S, BRANCH_W), BF16),
        scratch_shapes=[pltpu.VMEM((2, S, LANE), BF16), pltpu.VMEM((2, S, LANE), BF16), pltpu.VMEM((LANE, S), BF16),
                        pltpu.VMEM((2, S, TQ), F32), pltpu.VMEM((2, S, TQ), BF16)],
        compiler_params=_cparams(2),
        name="mixer_c",
    )(pn, pn, pn, cos_c, sin_c, q_norm, kv_norm, wuq, wuk, wuvt)


DB = 128
LSE_W = 16


def _mixd_kernel(q0_ref, q1_ref, q2_ref, k0_ref, k1_ref, k2_ref, v0_ref, v1_ref, v2_ref, cos_ref, sin_ref, o_ref,
                 nat, qs, ks, vts, onat, lnat):
    S = SEQ
    NB = S // DB
    UB = 4
    RB = UB * DB
    lane = lax.broadcasted_iota(jnp.int32, (RB, LANE), 1)
    lane_b = lax.broadcasted_iota(jnp.int32, (DB, LANE), 1)
    q_refs = (q0_ref, q1_ref, q2_ref)
    k_refs = (k0_ref, k1_ref, k2_ref)
    v_refs = (v0_ref, v1_ref, v2_ref)

    def nat_start(bi, d):
        nblk = NB // d
        return (bi % nblk) * (DB * d) + bi // nblk

    def reorder(d, src, dst):
        if d == 1:
            def direct(i, _):
                x = src(pl.multiple_of(i * RB, RB))
                for t in range(UB):
                    dst(i * UB + t, x[t * DB:(t + 1) * DB])
                return 0
            lax.fori_loop(0, NB // UB, direct, 0)
            return

        def stage(i, _):
            r0 = pl.multiple_of(i * RB, RB)
            nat[pl.ds(r0, RB), :] = src(r0)
            return 0
        lax.fori_loop(0, NB // UB, stage, 0)

        def gather(i, _):
            for t in range(UB):
                bi = i * UB + t
                dst(bi, nat[pl.ds(nat_start(bi, d), DB, stride=d), :])
            return 0
        lax.fori_loop(0, NB // UB, gather, 0)

    for g in range(len(DIL_GROUPS)):
        ks[g, 0:DB, :] = jnp.zeros((DB, LANE), BF16)
        vts[g, :, 0:DB] = jnp.zeros((LANE, DB), BF16)

    for g, (_, d) in enumerate(DIL_GROUPS):
        def q_src(r0, g=g):
            return _rope64(q_refs[g][pl.ds(r0, RB), :].astype(F32), cos_ref[pl.ds(r0, RB), :], sin_ref[pl.ds(r0, RB), :],
                           lane) * (DH_D ** -0.5 * LOG2E)

        def q_dst(bi, x, g=g):
            r0 = pl.multiple_of(bi * DB, DB)
            qs[g, 0, pl.ds(r0, DB), :] = jnp.where(lane_b < 64, x, 0.0).astype(BF16)
            qs[g, 1, pl.ds(r0, DB), :] = jnp.where(lane_b >= 64, x, 0.0).astype(BF16)

        def k_src(r0, g=g):
            return _rope64(k_refs[g][pl.ds(r0, RB), :].astype(F32), cos_ref[pl.ds(r0, RB), :], sin_ref[pl.ds(r0, RB), :],
                           lane)

        def k_dst(bi, x, g=g):
            ks[g, pl.ds(pl.multiple_of(DB + bi * DB, DB), DB), :] = x.astype(BF16)

        def v_src(r0, g=g):
            return v_refs[g][pl.ds(r0, RB), :].astype(F32)

        def v_dst(bi, x, g=g):
            vts[g, :, pl.ds(pl.multiple_of(DB + bi * DB, DB), DB)] = x.T.astype(BF16)

        reorder(d, q_src, q_dst)
        reorder(d, k_src, k_dst)
        reorder(d, v_src, v_dst)

    kk = lax.broadcasted_iota(jnp.int32, (2 * DB, DB), 0)
    qq = lax.broadcasted_iota(jnp.int32, (2 * DB, DB), 1) + DB

    def head(g, h, bi, nblk):
        r0 = pl.multiple_of(bi * DB, DB)
        s = _nt(ks[g, pl.ds(r0, 2 * DB), :], qs[g, h, pl.ds(r0, DB), :])
        yield
        lo = jnp.where(bi % nblk == 0, DB, 0)
        valid = (kk <= qq) & (kk >= jnp.maximum(qq - DB, lo))
        s = jnp.where(valid, s, -jnp.inf)
        m = jnp.max(s, axis=0, keepdims=True)
        p = jnp.exp2(s - m)
        l = jnp.sum(p, axis=0, keepdims=True)
        ot = _dot(vts[g, h * DH_D:(h + 1) * DH_D, pl.ds(r0, 2 * DB)], p.astype(BF16))
        yield
        yield ot / l, jnp.broadcast_to(m * (1.0 / LOG2E) + jnp.log(l), (DH_D, DB))

    for g, (_, d) in enumerate(DIL_GROUPS):
        def blocks(i, _, g=g, d=d):
            bis = [i * UB + t for t in range(UB)]
            gens = [head(g, h, bi, NB // d) for bi in bis for h in range(2)]
            for _ in range(2):
                for gen in gens:
                    next(gen)
            res = [next(gen) for gen in gens]
            for t, bi in enumerate(bis):
                (o0, l0), (o1, l1) = res[2 * t], res[2 * t + 1]
                start = nat_start(bi, d)
                rows = pl.ds(pl.multiple_of(start, DB), DB) if d == 1 else pl.ds(start, DB, stride=d)
                onat[g, rows, :] = jnp.concatenate([o0, o1], axis=0).T
                lnat[g, rows, :] = jnp.concatenate([l0, l1], axis=0).T
            return 0

        lax.fori_loop(0, NB // UB, blocks, 0)

    def merge(i, _):
        r0 = pl.multiple_of(i * DB, DB)
        ls = [lnat[g, pl.ds(r0, DB), :] for g in range(3)]
        mx = jnp.maximum(ls[0], jnp.maximum(ls[1], ls[2]))
        ws = [jnp.exp(l - mx) for l in ls]
        num = ws[0] * onat[0, pl.ds(r0, DB), :] + ws[1] * onat[1, pl.ds(r0, DB), :] + ws[2] * onat[2, pl.ds(r0, DB), :]
        o_ref[pl.ds(r0, DB), :] = (num / (ws[0] + ws[1] + ws[2])).astype(BF16)
        return 0

    lax.fori_loop(0, NB, merge, 0)


def mixer_d(pn, cos_a, sin_a):
    B = pn.shape[0] // SEQ
    S = SEQ
    ng = len(DIL_GROUPS)
    blk = lambda part, g: pl.BlockSpec((S, LANE), lambda b, p: (b, (PN_D + (part * ng + g) * BRANCH_W) // LANE + p))
    tab = pl.BlockSpec((S, LANE), lambda b, p: (b, 0))
    return pl.pallas_call(
        _mixd_kernel,
        grid=(B, H_D // 2),
        in_specs=[blk(part, g) for part in range(3) for g in range(ng)] + [tab, tab],
        out_specs=pl.BlockSpec((S, LANE), lambda b, p: (b, p)),
        out_shape=jax.ShapeDtypeStruct((B * S, BRANCH_W), BF16),
        scratch_shapes=[pltpu.VMEM((S, LANE), F32),
                        pltpu.VMEM((ng, 2, S, LANE), BF16),
                        pltpu.VMEM((ng, DB + S, LANE), BF16),
                        pltpu.VMEM((ng, LANE, DB + S), BF16),
                        pltpu.VMEM((ng, S, LANE), F32),
                        pltpu.VMEM((ng, S, LANE), F32)],
        compiler_params=_cparams(2),
        name="mixer_d",
    )(*([pn] * (3 * ng)), cos_a, sin_a)


GB = 256
NCHUNK = SEQ // CHUNK_B
PAD_B = 8


def _split3(x):
    hi = x.astype(BF16)
    r1 = x - hi.astype(F32)
    mid = r1.astype(BF16)
    lo = (r1 - mid.astype(F32)).astype(BF16)
    return hi, mid, lo


def _mixb_kernel(qkv_ref, misc_ref, conv_ref, alog_ref, dt_ref, ng_ref, o_ref,
                 xpad, c_scr, gc_scr, gl_scr, gct_scr, qeff_scr, o0_scr, n_scr, g_scr, s_scr):
    S = SEQ
    R = GB
    NG = S // R
    ri = lax.broadcasted_iota(jnp.int32, (R, R), 0)
    ci = lax.broadcasted_iota(jnp.int32, (R, R), 1)
    same = (ri // CHUNK_B) == (ci // CHUNK_B)
    tril = same & (ri >= ci)
    strict = same & (ri > ci)
    eye = jnp.where(ri == ci, 1.0, 0.0)

    sel = jnp.concatenate([jnp.where(tril, 1.0, 0.0), jnp.where(same, 1.0, 0.0)], axis=0).astype(BF16)
    neg_a = -jnp.exp(alog_ref[...])

    def gprep(i, _):
        r0 = pl.multiple_of(i * R, R)
        x = misc_ref[pl.ds(r0, R), :].astype(F32) + dt_ref[...]
        sp = jnp.maximum(x, 0.0) + jnp.log1p(jnp.exp(-jnp.abs(x)))
        g = neg_a * sp
        hi, mid, lo = _split3(g)
        acc = _dot(sel, jnp.concatenate([hi, mid, lo], axis=1))
        acc = acc[:, 0:LANE] + acc[:, LANE:2 * LANE] + acc[:, 2 * LANE:3 * LANE]
        gc_scr[pl.ds(r0, R), :] = acc[:R]
        gl_scr[pl.ds(r0, R), :] = acc[R:]
        gct_scr[i] = acc[:R].T[0:8, :]
        return 0

    lax.fori_loop(0, NG, gprep, 0)

    xpad[0:PAD_B, :] = jnp.zeros((PAD_B, LANE), F32)

    def conv_head(h):
        for t in range(3):
            cb = t * H_B + h
            cs = slice(cb * LANE, (cb + 1) * LANE)

            def fill(i, _):
                r0 = pl.multiple_of(i * R, R)
                xpad[pl.ds(PAD_B + r0, R), :] = qkv_ref[pl.ds(r0, R), cs].astype(F32)
                return 0

            lax.fori_loop(0, NG, fill, 0)

            def conv(i, _):
                r0 = pl.multiple_of(i * R, R)
                y = jnp.zeros((R, LANE), F32)
                for j in range(CONV_B):
                    y = y + xpad[pl.ds(r0 + (PAD_B - (CONV_B - 1) + j), R), :] * conv_ref[j:j + 1, cs]
                y = _silu(y)
                if t < 2:
                    y = y * lax.rsqrt(jnp.sum(y * y, axis=-1, keepdims=True) + NORM_EPS)
                c_scr[pl.ds(r0, R), (3 * h + t) * LANE:(3 * h + t + 1) * LANE] = y
                return 0

            lax.fori_loop(0, NG, conv, 0)

    def group_head(i, h):
        r0 = pl.multiple_of(i * R, R)
        qn = c_scr[pl.ds(r0, R), (3 * h) * LANE:(3 * h + 1) * LANE]
        kn = c_scr[pl.ds(r0, R), (3 * h + 1) * LANE:(3 * h + 2) * LANE]
        v = c_scr[pl.ds(r0, R), (3 * h + 2) * LANE:(3 * h + 3) * LANE]
        mf = misc_ref[pl.ds(r0, R), :].astype(F32)
        beta = _sigmoid(mf[:, MISC_BETA + h:MISC_BETA + h + 1])
        gcol = gc_scr[pl.ds(r0, R), MISC_DECAY + h:MISC_DECAY + h + 1]
        glcol = gl_scr[pl.ds(r0, R), MISC_DECAY + h:MISC_DECAY + h + 1]
        grow = gct_scr[i, MISC_DECAY + h:MISC_DECAY + h + 1, :]
        dec = jnp.exp(jnp.where(tril, gcol - grow, 0.0))
        kb = kn * beta
        knb = kn.astype(BF16)
        lm = jnp.where(strict, _nt(kb.astype(BF16), knb) * dec, 0.0)
        qkm = jnp.where(tril, _nt((qn * (DK_B ** -0.5)).astype(BF16), knb) * dec, 0.0)
        yield
        x = eye - jnp.where((ri // 2) == (ci // 2), lm, 0.0)
        s = 2
        while s < CHUNK_B:
            off = ((ri // (2 * s)) == (ci // (2 * s))) & ((ri // s) != (ci // s))
            xb = x.astype(BF16)
            t = _dot(jnp.where(off, lm, 0.0).astype(BF16), xb).astype(BF16)
            yield
            x = x - _dot(xb, t)
            yield
            s *= 2
        rhs = jnp.concatenate([v * beta, kb * jnp.exp(gcol)], axis=1).astype(BF16)
        uw = _dot(x.astype(BF16), rhs)
        yield
        uwb = uw.astype(BF16)
        ox = _dot(qkm.astype(BF16), uwb)
        yield
        qeff_scr[h, pl.ds(r0, R), :] = (qn * (DK_B ** -0.5) * jnp.exp(gcol) - ox[:, LANE:]).astype(BF16)
        o0_scr[h, pl.ds(r0, R), :] = ox[:, :LANE]
        kt = (kn * jnp.exp(glcol - gcol)).astype(BF16)
        for c in range(R // CHUNK_B):
            gn = _tn(kt[c * CHUNK_B:(c + 1) * CHUNK_B], uwb[c * CHUNK_B:(c + 1) * CHUNK_B])
            n_scr[h, i * (R // CHUNK_B) + c] = gn[:, :LANE]
            g_scr[h, i * (R // CHUNK_B) + c] = gn[:, LANE:].astype(BF16)

    for h in range(H_B):
        conv_head(h)

    def group(i, _):
        gens = [group_head(i, h) for h in range(H_B)]
        while gens:
            gens = [g for g in gens if next(g, True) is None]
        return 0

    lax.fori_loop(0, NG, group, 0)

    s_scr[...] = jnp.zeros(s_scr.shape, F32)

    def scan(c, _):
        r0 = pl.multiple_of(c * CHUNK_B, CHUNK_B)
        glrow = jnp.exp(gl_scr[pl.ds(r0, 1), :])
        for h in range(H_B):
            st = s_scr[h]
            sb = st.astype(BF16)
            o = _dot(qeff_scr[h, pl.ds(r0, CHUNK_B), :], sb) + o0_scr[h, pl.ds(r0, CHUNK_B), :]
            o = _rmsnorm_rows(o, ng_ref[...])
            o_ref[pl.ds(r0, CHUNK_B), h * LANE:(h + 1) * LANE] = o.astype(BF16)
            a = glrow[:, MISC_DECAY + h:MISC_DECAY + h + 1]
            s_scr[h] = a * st - _dot(g_scr[h, c], sb) + n_scr[h, c]
        return 0

    lax.fori_loop(0, NCHUNK, scan, 0)


def mixer_b(pn, conv_w, alog_row, dt_row, out_norm):
    B = pn.shape[0] // SEQ
    S = SEQ
    W = 3 * H_B * DK_B
    row = pl.BlockSpec((1, LANE), lambda b: (0, 0))
    return pl.pallas_call(
        _mixb_kernel,
        grid=(B,),
        in_specs=[pl.BlockSpec((S, W), lambda b: (b, PN_B // W), pipeline_mode=pl.Buffered(1)),
                  pl.BlockSpec((S, LANE), lambda b: (b, PN_MISC // LANE)),
                  pl.BlockSpec((CONV_B, W), lambda b: (0, 0)), row, row, row],
        out_specs=pl.BlockSpec((S, BRANCH_W), lambda b: (b, 0)),
        out_shape=jax.ShapeDtypeStruct((B * S, BRANCH_W), BF16),
        scratch_shapes=[pltpu.VMEM((PAD_B + S, LANE), F32),
                        pltpu.VMEM((S, 3 * H_B * LANE), F32),
                        pltpu.VMEM((S, LANE), F32),
                        pltpu.VMEM((S, LANE), F32),
                        pltpu.VMEM((S // GB, 8, GB), F32),
                        pltpu.VMEM((H_B, S, LANE), BF16),
                        pltpu.VMEM((H_B, S, LANE), F32),
                        pltpu.VMEM((H_B, NCHUNK, DK_B, DV_B), F32),
                        pltpu.VMEM((H_B, NCHUNK, DK_B, DK_B), BF16),
                        pltpu.VMEM((H_B, DK_B, DV_B), F32)],
        compiler_params=_cparams(1),
        name="mixer_b",
    )(pn, pn, conv_w, alog_row, dt_row, out_norm)


TM_MERGE = 512


def _merge_kernel(*refs):
    o_refs = refs[0:4]
    z_refs = refs[4:8]
    m_refs = refs[8:12]
    x_ref, mod_ref, wbr_ref, wout_ref, lng_ref, lnb_ref, out_ref = refs[12:]
    merged = jnp.zeros((TM_MERGE, D_MODEL), F32)
    for n in range(N_BRANCH):
        br = (o_refs[n][...].astype(F32) * _silu(z_refs[n][...].astype(F32))).astype(BF16)
        merged = merged + _dot(br, wbr_ref[n]) * _sigmoid(m_refs[n][...].astype(F32))
    y = _dot(merged.astype(BF16), wout_ref[...])
    t = DEEPNORM_ALPHA * x_ref[...] + mod_ref[0, 2:3, :] * y
    mu = jnp.mean(t, axis=-1, keepdims=True)
    tc = t - mu
    var = jnp.mean(tc * tc, axis=-1, keepdims=True)
    out_ref[...] = tc * lax.rsqrt(var + NORM_EPS) * lng_ref[...] + lnb_ref[...]


def merge_out(pn, o_a, o_b, o_c, o_d, x2, mod, w_br, w_out, ln_g, ln_b):
    T = x2.shape[0]
    tm = TM_MERGE
    per_b = SEQ // tm
    br = pl.BlockSpec((tm, BRANCH_W), lambda i: (i, 0))
    zspec = lambda n: pl.BlockSpec((tm, BRANCH_W), lambda i: (i, PN_Z // BRANCH_W + n))
    mspec = lambda n: pl.BlockSpec((tm, D_MODEL), lambda i: (i, PN_MERGE // D_MODEL + n))
    row = pl.BlockSpec((1, D_MODEL), lambda i: (0, 0))
    return pl.pallas_call(
        _merge_kernel,
        grid=(T // tm,),
        in_specs=[br, br, br, br] + [zspec(n) for n in range(4)] + [mspec(n) for n in range(4)] + [
            pl.BlockSpec((tm, D_MODEL), lambda i: (i, 0)),
            pl.BlockSpec((1, 3, D_MODEL), lambda i: (i // per_b, 0, 0)),
            pl.BlockSpec((N_BRANCH, BRANCH_W, D_MODEL), lambda i: (0, 0, 0)),
            pl.BlockSpec((D_MODEL, D_MODEL), lambda i: (0, 0)), row, row],
        out_specs=pl.BlockSpec((tm, D_MODEL), lambda i: (i, 0)),
        out_shape=jax.ShapeDtypeStruct((T, D_MODEL), F32),
        compiler_params=_cparams(1),
        name="merge_out",
    )(o_a, o_b, o_c, o_d, pn, pn, pn, pn, pn, pn, pn, pn, x2, mod, w_br, w_out, ln_g, ln_b)


def _prep_w_in(w):
    zc = lambda n: jnp.zeros((D_MODEL, n), w.dtype)
    misc = jnp.concatenate([w[:, _OFF_BETA:_OFF_BETA + 4], w[:, _OFF_DECAY:_OFF_DECAY + 4], zc(MISC_KPE - 8),
                            w[:, _OFF_KPE:_OFF_KPE + ROPE_C]], axis=1)
    wn = jnp.concatenate([w[:, _OFF_A:_OFF_A + 1536], w[:, _OFF_BQKV:_OFF_BQKV + 1536],
                          w[:, _OFF_CQ:_OFF_CQ + Q_LORA_C], w[:, _OFF_CKV:_OFF_CKV + KV_LORA_C], misc,
                          w[:, _OFF_D:_OFF_Z], w[:, _OFF_Z:_OFF_Z + 2048], w[:, _OFF_MERGE:_OFF_MERGE + 4096]],
                         axis=1)
    return wn.astype(BF16)


def _prep_w_c(w_uq, w_ukv):
    q = w_uq.reshape(Q_LORA_C, H_C, NOPE_C + ROPE_C)
    wq = jnp.concatenate([q[..., :NOPE_C], jnp.zeros((Q_LORA_C, H_C, LANE - NOPE_C - ROPE_C), q.dtype),
                          q[..., NOPE_C:]], axis=-1).reshape(Q_LORA_C, H_C * LANE)
    kv = w_ukv.reshape(KV_LORA_C, H_C, NOPE_C + V_C)
    wk = jnp.concatenate([kv[..., :NOPE_C], jnp.zeros((KV_LORA_C, H_C, LANE - NOPE_C), kv.dtype)],
                         axis=-1).reshape(KV_LORA_C, H_C * LANE)
    wvt = kv[..., NOPE_C:].reshape(KV_LORA_C, H_C * V_C).T
    return wq.astype(BF16), wk.astype(BF16), wvt.astype(BF16)


def _lane_row(vals, at):
    return jnp.zeros((1, LANE), F32).at[0, at:at + vals.shape[0]].set(vals.astype(F32))


def kernel(x, c, positions, w_ada, b_ada, w_in, conv_b, a_log, dt_bias, out_norm_b, lambda_q1, lambda_k1,
           lambda_q2, lambda_k2, subln_g, q_norm_c, w_uq, kv_norm_c, w_ukv, w_br, w_out, ln_g, ln_b):
    B, S, D = x.shape
    assert (S, D) == (SEQ, D_MODEL) and w_in.shape[0] == DEPTH
    T = B * S
    mods = ada_modulation(c, w_ada, b_ada)
    cos_a, sin_a, cos_c, sin_c = rope_tables(positions)
    x2 = x.reshape(T, D)
    for l in range(DEPTH):
        mod = mods[l]
        pn = input_projection(x2, mod, _prep_w_in(w_in[l]))
        lam = jnp.stack([lambda_q1[l], lambda_k1[l], lambda_q2[l], lambda_k2[l]]).astype(F32)
        o_a = mixer_a(pn, cos_a, sin_a, lam, subln_g[l].reshape(LANE, 1),
                      lam_init=0.8 - 0.6 * math.exp(-0.3 * l))
        o_b = mixer_b(pn, conv_b[l], _lane_row(a_log[l], MISC_DECAY), _lane_row(dt_bias[l], MISC_DECAY),
                      out_norm_b[l].reshape(1, LANE))
        wq, wk, wv = _prep_w_c(w_uq[l], w_ukv[l])
        o_c = mixer_c(pn, cos_c, sin_c, q_norm_c[l].reshape(1, Q_LORA_C), kv_norm_c[l].reshape(1, KV_LORA_C),
                      wq, wk, wv)
        o_d = mixer_d(pn, cos_a, sin_a)
        x2 = merge_out(pn, o_a, o_b, o_c, o_d, x2, mod, w_br[l].astype(BF16), w_out[l].astype(BF16),
                       ln_g[l].reshape(1, D), ln_b[l].reshape(1, D))
    return x2.reshape(B, S, D)
```

```python
import functools
import math

import jax
import jax.numpy as jnp
from jax import lax
from jax.experimental import pallas as pl
from jax.experimental.pallas import tpu as pltpu

F32 = jnp.float32
BF16 = jnp.bfloat16

D_MODEL = 1024
SEQ = 2048
DEPTH = 2
ROPE_THETA = 500000.0
NORM_EPS = 1e-6
H_A, DH_A = 4, 64
H_B, DK_B, DV_B, CONV_B, CHUNK_B = 4, 128, 128, 4, 64
H_C, Q_LORA_C, KV_LORA_C, NOPE_C, ROPE_C, V_C = 8, 256, 128, 64, 32, 64
H_D, DH_D = 8, 64
DIL_GROUPS = ((128, 1), (512, 4), (2048, 16))
N_BRANCH, BRANCH_W = 4, 512
DEEPNORM_ALPHA = (2.0 * DEPTH) ** 0.25

LANE = 128
VMEM_LIMIT = 56 * 1024 * 1024

_OFF_A = 0
_OFF_BQKV = 1536
_OFF_BETA = 3072
_OFF_DECAY = 3076
_OFF_CQ = 3080
_OFF_CKV = 3336
_OFF_KPE = 3464
_OFF_D = 3496
_OFF_Z = 8104
_OFF_MERGE = 10152
D_IN = 14248

PN_A = 0
PN_B = 1536
PN_CQ = 3072
PN_CKV = 3328
PN_MISC = 3456
PN_D = 3584
PN_Z = 8192
PN_MERGE = 10240
PN_W = 14336
MISC_BETA, MISC_DECAY, MISC_KPE = 0, 4, 96


def _cparams(n_grid):
    return pltpu.CompilerParams(dimension_semantics=("arbitrary",) * n_grid,
                                vmem_limit_bytes=VMEM_LIMIT)


def _nt(a, b):
    return lax.dot_general(a, b, (((1,), (1,)), ((), ())), preferred_element_type=F32)


def _tn(a, b):
    return lax.dot_general(a, b, (((0,), (0,)), ((), ())), preferred_element_type=F32)


def _dot(a, b):
    return jnp.dot(a, b, preferred_element_type=F32)


def _sigmoid(x):
    return 1.0 / (1.0 + jnp.exp(-x))


def _silu(x):
    return x * _sigmoid(x)


def _ada_kernel(c_ref, w_ref, b_ref, o_ref):
    ca = _silu(c_ref[...]).astype(BF16)
    o_ref[0] = _dot(ca, w_ref[0].astype(BF16)) + b_ref[0]


def ada_modulation(c, w_ada, b_ada):
    B = c.shape[0]
    L = w_ada.shape[0]
    out = pl.pallas_call(
        _ada_kernel,
        grid=(L, 3),
        in_specs=[pl.BlockSpec((B, D_MODEL), lambda l, j: (0, 0)),
                  pl.BlockSpec((1, D_MODEL, D_MODEL), lambda l, j: (l, 0, j)),
                  pl.BlockSpec((1, 1, D_MODEL), lambda l, j: (l, 0, j))],
        out_specs=pl.BlockSpec((1, B, D_MODEL), lambda l, j: (l, 0, j)),
        out_shape=jax.ShapeDtypeStruct((L, B, 3 * D_MODEL), F32),
        compiler_params=_cparams(2),
        name="ada_modulation",
    )(c, w_ada, b_ada.reshape(L, 1, 3 * D_MODEL))
    return out.reshape(L, B, 3, D_MODEL)


def _rope_rows():
    half_a = (DH_A // 4) // 2
    inv_a = ROPE_THETA ** (-jnp.arange(half_a, dtype=F32) / half_a)
    half_c = ROPE_C // 2
    inv_c = ROPE_THETA ** (-jnp.arange(half_c, dtype=F32) / half_c)
    z = jnp.zeros
    inv = jnp.concatenate([inv_a, inv_a, z(48, F32), inv_a, inv_a, z(16, F32), inv_c, inv_c])
    sgn = jnp.concatenate([-jnp.ones(8, F32), jnp.ones(8, F32), z(48, F32),
                           -jnp.ones(8, F32), jnp.ones(8, F32), z(16, F32),
                           -jnp.ones(16, F32), jnp.ones(16, F32)])
    return inv.reshape(1, LANE), sgn.reshape(1, LANE)


def _rope_table_kernel(pos_ref, inv_ref, sgn_ref, cos_a, sin_a, cos_c, sin_c):
    R = 256

    def body(i, _):
        r0 = pl.multiple_of(i * R, R)
        ang = pos_ref[pl.ds(r0, R), :].astype(F32) * inv_ref[...]
        cv = jnp.cos(ang)
        sv = jnp.sin(ang) * sgn_ref[...]
        lane = lax.broadcasted_iota(jnp.int32, (R, LANE), 1)
        is_c = lane >= MISC_KPE
        cos_a[pl.ds(r0, R), :] = jnp.where(is_c, 1.0, cv)
        sin_a[pl.ds(r0, R), :] = jnp.where(is_c, 0.0, sv)
        cos_c[pl.ds(r0, R), :] = jnp.where(is_c, cv, 1.0)
        sin_c[pl.ds(r0, R), :] = jnp.where(is_c, sv, 0.0)
        return 0

    lax.fori_loop(0, SEQ // R, body, 0)


def rope_tables(positions):
    B, S = positions.shape
    inv, sgn = _rope_rows()
    tab = jax.ShapeDtypeStruct((B * S, LANE), F32)
    row = pl.BlockSpec((1, LANE), lambda b: (0, 0))
    blk = pl.BlockSpec((S, LANE), lambda b: (b, 0))
    return pl.pallas_call(
        _rope_table_kernel,
        grid=(B,),
        in_specs=[pl.BlockSpec((S, 1), lambda b: (b, 0)), row, row],
        out_specs=[blk, blk, blk, blk],
        out_shape=[tab, tab, tab, tab],
        compiler_params=_cparams(1),
        name="rope_tables",
    )(positions.reshape(B * S, 1), inv, sgn)


TM_IN = 1024
TN_IN = 2048


def _inproj_kernel(x_ref, mod_ref, w_ref, o_ref, h_ref):
    @pl.when(pl.program_id(1) == 0)
    def _():
        shift = mod_ref[0, 0:1, :]
        scale1 = 1.0 + mod_ref[0, 1:2, :]
        R = 128

        def body(i, _):
            r0 = pl.multiple_of(i * R, R)
            xs = x_ref[pl.ds(r0, R), :]
            mu = jnp.mean(xs, axis=-1, keepdims=True)
            xc = xs - mu
            var = jnp.mean(xc * xc, axis=-1, keepdims=True)
            hn = xc * lax.rsqrt(var + NORM_EPS)
            h_ref[pl.ds(r0, R), :] = (hn * scale1 + shift).astype(BF16)
            return 0

        lax.fori_loop(0, TM_IN // R, body, 0)

    o_ref[...] = _dot(h_ref[...], w_ref[...]).astype(BF16)


def input_projection(x2, mod, w):
    T = x2.shape[0]
    ncol = w.shape[1]
    return pl.pallas_call(
        _inproj_kernel,
        grid=(T // TM_IN, ncol // TN_IN),
        in_specs=[pl.BlockSpec((TM_IN, D_MODEL), lambda i, j: (i, 0)),
                  pl.BlockSpec((1, 3, D_MODEL), lambda i, j: (i // (SEQ // TM_IN), 0, 0)),
                  pl.BlockSpec((D_MODEL, TN_IN), lambda i, j: (0, j))],
        out_specs=pl.BlockSpec((TM_IN, TN_IN), lambda i, j: (i, j)),
        out_shape=jax.ShapeDtypeStruct((T, ncol), BF16),
        scratch_shapes=[pltpu.VMEM((TM_IN, D_MODEL), BF16)],
        compiler_params=_cparams(2),
        name="input_projection",
    )(x2, mod, w)


def _rope64(x, cos, sin, lane):
    sw = jnp.where((lane & 63) < 8, pltpu.roll(x, LANE - 8, 1), pltpu.roll(x, 8, 1))
    return x * cos + sw * sin


TQ = 256
TKB = 256
LOG2E = 1.4426950408889634


def _attention_pipeline(nq, nsub, operands, finish):
    items = [(qi, sub) for qi in range(nq) for sub in range(nsub)]
    state = {}

    def phase_a(it):
        qi, _ = it
        k_ref, q_blk, _, s_ref, _ = operands(*it)
        kmax = (qi + 1) * TQ
        m = None
        for c0 in range(0, kmax, 2 * TKB):
            c1 = min(c0 + 2 * TKB, kmax)
            s = _nt(k_ref[c0:c1, :], q_blk)
            if c1 == kmax:
                kk = lax.broadcasted_iota(jnp.int32, (c1 - c0, TQ), 0) + (c0 - qi * TQ)
                qq = lax.broadcasted_iota(jnp.int32, (c1 - c0, TQ), 1)
                s = jnp.where(kk <= qq, s, -jnp.inf)
            s_ref[c0:c1, :] = s
            mc = jnp.max(s, axis=0, keepdims=True)
            m = mc if m is None else jnp.maximum(m, mc)
            yield
        state[it] = {"m": m}

    def phase_b(it):
        qi, _ = it
        _, _, _, s_ref, p_ref = operands(*it)
        m = state[it]["m"]
        l = None
        for t in range(qi + 1):
            p = jnp.exp2(s_ref[t * TKB:(t + 1) * TKB, :] - m)
            p_ref[t * TKB:(t + 1) * TKB, :] = p.astype(BF16)
            lt = jnp.sum(p, axis=0, keepdims=True)
            l = lt if l is None else l + lt
            yield
        state[it]["l"] = l

    def phase_c(it):
        qi, _ = it
        _, _, vt_ref, _, p_ref = operands(*it)
        kmax = (qi + 1) * TQ
        acc = None
        for c0 in range(0, kmax, 2 * TKB):
            c1 = min(c0 + 2 * TKB, kmax)
            d = _dot(vt_ref[:, c0:c1], p_ref[c0:c1, :])
            acc = d if acc is None else acc + d
            yield
        state[it]["o"] = acc / state[it]["l"]

    n = len(items)
    for step in range(n + 2):
        gens = []
        if step < n:
            gens.append(phase_a(items[step]))
        if 0 <= step - 1 < n:
            gens.append(phase_b(items[step - 1]))
        if 0 <= step - 2 < n:
            gens.append(phase_c(items[step - 2]))
        while gens:
            gens = [g for g in gens if next(g, True) is None]
        done = step - 2
        if done >= 0 and items[done][1] == nsub - 1:
            qi = items[done][0]
            finish(qi, [state.pop((qi, sub))["o"] for sub in range(nsub)])


def _mixa_kernel(q_ref, k_ref, v_ref, cos_ref, sin_ref, lam_ref, g_ref, o_ref, q2_ref, kr_ref, vt_ref, s_ref, p_ref,
                 *, lam_init):
    S = SEQ
    R = 128

    def prep(i, _):
        r0 = pl.multiple_of(i * R, R)
        lane = lax.broadcasted_iota(jnp.int32, (R, LANE), 1)
        cos = cos_ref[pl.ds(r0, R), :]
        sin = sin_ref[pl.ds(r0, R), :]
        q = _rope64(q_ref[pl.ds(r0, R), :].astype(F32), cos, sin, lane) * (DH_A ** -0.5 * LOG2E)
        q2_ref[0, pl.ds(r0, R), :] = jnp.where(lane < 64, q, 0.0).astype(BF16)
        q2_ref[1, pl.ds(r0, R), :] = jnp.where(lane >= 64, q, 0.0).astype(BF16)
        kr_ref[pl.ds(r0, R), :] = _rope64(k_ref[pl.ds(r0, R), :].astype(F32), cos, sin, lane).astype(BF16)
        vt_ref[:, pl.ds(r0, R)] = v_ref[pl.ds(r0, R), :].astype(F32).T.astype(BF16)
        return 0

    lax.fori_loop(0, S // R, prep, 0)

    lp = lam_ref[...]
    lam = (jnp.exp(jnp.sum(lp[0:1] * lp[1:2], axis=-1, keepdims=True))
           - jnp.exp(jnp.sum(lp[2:3] * lp[3:4], axis=-1, keepdims=True)) + lam_init)
    gcol = g_ref[...] * (1.0 - lam_init)

    def finish(qi, outs):
        o = outs[0] - lam * outs[1]
        ms = jnp.mean(o * o, axis=0, keepdims=True)
        o = o * lax.rsqrt(ms + NORM_EPS) * gcol
        o_ref[qi * TQ:(qi + 1) * TQ, :] = o.T.astype(BF16)

    _attention_pipeline(S // TQ, 2,
                        lambda qi, mp: (kr_ref, q2_ref[mp, qi * TQ:(qi + 1) * TQ, :], vt_ref, s_ref.at[mp], p_ref.at[mp]),
                        finish)


def mixer_a(pn, cos_a, sin_a, lam_params, subln_g_col, *, lam_init=0.2):
    B = pn.shape[0] // SEQ
    S = SEQ
    col = lambda off: (lambda b, h: (b, off // LANE + h))
    blk = lambda off: pl.BlockSpec((S, LANE), col(off))
    tab = pl.BlockSpec((S, LANE), lambda b, h: (b, 0))
    return pl.pallas_call(
        functools.partial(_mixa_kernel, lam_init=lam_init),
        grid=(B, H_A),
        in_specs=[blk(PN_A), blk(PN_A + 512), blk(PN_A + 1024), tab, tab,
                  pl.BlockSpec((4, DH_A), lambda b, h: (0, 0)),
                  pl.BlockSpec((LANE, 1), lambda b, h: (0, 0))],
        out_specs=pl.BlockSpec((S, LANE), lambda b, h: (b, h)),
        out_shape=jax.ShapeDtypeStruct((B * S, BRANCH_W), BF16),
        scratch_shapes=[pltpu.VMEM((2, S, LANE), BF16), pltpu.VMEM((S, LANE), BF16), pltpu.VMEM((LANE, S), BF16),
                        pltpu.VMEM((2, S, TQ), F32), pltpu.VMEM((2, S, TQ), BF16)],
        compiler_params=_cparams(2),
        name="mixer_a",
    )(pn, pn, pn, cos_a, sin_a, lam_params, subln_g_col)


def _rmsnorm_rows(x, g):
    return x * lax.rsqrt(jnp.mean(x * x, axis=-1, keepdims=True) + NORM_EPS) * g


def _rope_c(x, cos, sin, lane):
    sw = jnp.where(lane < 112, pltpu.roll(x, LANE - 16, 1), pltpu.roll(x, 16, 1))
    return x * cos + sw * sin


def _mixc_kernel(cq_ref, ckv_ref, misc_ref, cos_ref, sin_ref, qg_ref, kvg_ref, wuq_ref, wuk_ref, wuvt_ref,
                 o_ref, q_scr, k_scr, vt_scr, s_ref, p_ref):
    S = SEQ
    R = 256
    scale = (NOPE_C + ROPE_C) ** -0.5 * LOG2E

    def prep(i, _):
        r0 = pl.multiple_of(i * R, R)
        lane = lax.broadcasted_iota(jnp.int32, (R, LANE), 1)
        cos = cos_ref[pl.ds(r0, R), :]
        sin = sin_ref[pl.ds(r0, R), :]
        cqn = _rmsnorm_rows(cq_ref[pl.ds(r0, R), :].astype(F32), qg_ref[...]).astype(BF16)
        q = _dot(cqn, wuq_ref[...])
        kvn = _rmsnorm_rows(ckv_ref[pl.ds(r0, R), :].astype(F32), kvg_ref[...]).astype(BF16)
        kk = _dot(kvn, wuk_ref[...])
        kpe = jnp.where(lane >= MISC_KPE, misc_ref[pl.ds(r0, R), :].astype(F32), 0.0)
        kpe = _rope_c(kpe, cos, sin, lane)
        for e in range(2):
            qe = _rope_c(q[:, e * LANE:(e + 1) * LANE], cos, sin, lane) * scale
            q_scr[e, pl.ds(r0, R), :] = qe.astype(BF16)
            k_scr[e, pl.ds(r0, R), :] = (kk[:, e * LANE:(e + 1) * LANE] + kpe).astype(BF16)
        vt_scr[:, pl.ds(r0, R)] = _nt(wuvt_ref[...], kvn).astype(BF16)
        return 0

    lax.fori_loop(0, S // R, prep, 0)

    def finish(qi, outs):
        o_ref[qi * TQ:(qi + 1) * TQ, :] = jnp.concatenate(outs, axis=0).T.astype(BF16)

    _attention_pipeline(S // TQ, 2,
                        lambda qi, e: (k_scr.at[e], q_scr[e, qi * TQ:(qi + 1) * TQ, :],
                                       vt_scr.at[e * V_C:(e + 1) * V_C], s_ref.at[e], p_ref.at[e]),
                        finish)


def mixer_c(pn, cos_c, sin_c, q_norm, kv_norm, wuq, wuk, wuvt):
    B = pn.shape[0] // SEQ
    S = SEQ
    tab = pl.BlockSpec((S, LANE), lambda b, p: (b, 0))
    return pl.pallas_call(
        _mixc_kernel,
        grid=(B, H_C // 2),
        in_specs=[pl.BlockSpec((S, Q_LORA_C), lambda b, p: (b, PN_CQ // Q_LORA_C)),
                  pl.BlockSpec((S, LANE), lambda b, p: (b, PN_CKV // LANE)),
                  pl.BlockSpec((S, LANE), lambda b, p: (b, PN_MISC // LANE)),
                  tab, tab,
                  pl.BlockSpec((1, Q_LORA_C), lambda b, p: (0, 0)),
                  pl.BlockSpec((1, KV_LORA_C), lambda b, p: (0, 0)),
                  pl.BlockSpec((Q_LORA_C, 2 * LANE), lambda b, p: (0, p)),
                  pl.BlockSpec((KV_LORA_C, 2 * LANE), lambda b, p: (0, p)),
                  pl.BlockSpec((LANE, KV_LORA_C), lambda b, p: (p, 0))],
        out_specs=pl.BlockSpec((S, LANE), lambda b, p: (b, p)),
        out_shape=jax.ShapeDtypeStruct((B * S, BRANCH_W), BF16),
        scratch_shapes=[pltpu.VMEM((2, S, LANE), BF16), pltpu.VMEM((2, S, LANE), BF16), pltpu.VMEM((LANE, S), BF16),
                        pltpu.VMEM((2, S, TQ), F32), pltpu.VMEM((2, S, TQ), BF16)],
        compiler_params=_cparams(2),
        name="mixer_c",
    )(pn, pn, pn, cos_c, sin_c, q_norm, kv_norm, wuq, wuk, wuvt)


DB = 128
LSE_W = 16


def _mixd_kernel(q0_ref, q1_ref, q2_ref, k0_ref, k1_ref, k2_ref, v0_ref, v1_ref, v2_ref, cos_ref, sin_ref, o_ref,
                 nat, qs, ks, vts, onat, lnat):
    S = SEQ
    NB = S // DB
    UB = 4
    RB = UB * DB
    lane = lax.broadcasted_iota(jnp.int32, (RB, LANE), 1)
    lane_b = lax.broadcasted_iota(jnp.int32, (DB, LANE), 1)
    q_refs = (q0_ref, q1_ref, q2_ref)
    k_refs = (k0_ref, k1_ref, k2_ref)
    v_refs = (v0_ref, v1_ref, v2_ref)

    def nat_start(bi, d):
        nblk = NB // d
        return (bi % nblk) * (DB * d) + bi // nblk

    def reorder(d, src, dst):
        if d == 1:
            def direct(i, _):
                x = src(pl.multiple_of(i * RB, RB))
                for t in range(UB):
                    dst(i * UB + t, x[t * DB:(t + 1) * DB])
                return 0
            lax.fori_loop(0, NB // UB, direct, 0)
            return

        def stage(i, _):
            r0 = pl.multiple_of(i * RB, RB)
            nat[pl.ds(r0, RB), :] = src(r0)
            return 0
        lax.fori_loop(0, NB // UB, stage, 0)

        def gather(i, _):
            for t in range(UB):
                bi = i * UB + t
                dst(bi, nat[pl.ds(nat_start(bi, d), DB, stride=d), :])
            return 0
        lax.fori_loop(0, NB // UB, gather, 0)

    for g in range(len(DIL_GROUPS)):
        ks[g, 0:DB, :] = jnp.zeros((DB, LANE), BF16)
        vts[g, :, 0:DB] = jnp.zeros((LANE, DB), BF16)

    for g, (_, d) in enumerate(DIL_GROUPS):
        def q_src(r0, g=g):
            return _rope64(q_refs[g][pl.ds(r0, RB), :].astype(F32), cos_ref[pl.ds(r0, RB), :], sin_ref[pl.ds(r0, RB), :],
                           lane) * (DH_D ** -0.5 * LOG2E)

        def q_dst(bi, x, g=g):
            r0 = pl.multiple_of(bi * DB, DB)
            qs[g, 0, pl.ds(r0, DB), :] = jnp.where(lane_b < 64, x, 0.0).astype(BF16)
            qs[g, 1, pl.ds(r0, DB), :] = jnp.where(lane_b >= 64, x, 0.0).astype(BF16)

        def k_src(r0, g=g):
            return _rope64(k_refs[g][pl.ds(r0, RB), :].astype(F32), cos_ref[pl.ds(r0, RB), :], sin_ref[pl.ds(r0, RB), :],
                           lane)

        def k_dst(bi, x, g=g):
            ks[g, pl.ds(pl.multiple_of(DB + bi * DB, DB), DB), :] = x.astype(BF16)

        def v_src(r0, g=g):
            return v_refs[g][pl.ds(r0, RB), :].astype(F32)

        def v_dst(bi, x, g=g):
            vts[g, :, pl.ds(pl.multiple_of(DB + bi * DB, DB), DB)] = x.T.astype(BF16)

        reorder(d, q_src, q_dst)
        reorder(d, k_src, k_dst)
        reorder(d, v_src, v_dst)

    kk = lax.broadcasted_iota(jnp.int32, (2 * DB, DB), 0)
    qq = lax.broadcasted_iota(jnp.int32, (2 * DB, DB), 1) + DB

    def head(g, h, bi, nblk):
        r0 = pl.multiple_of(bi * DB, DB)
        s = _nt(ks[g, pl.ds(r0, 2 * DB), :], qs[g, h, pl.ds(r0, DB), :])
        yield
        lo = jnp.where(bi % nblk == 0, DB, 0)
        valid = (kk <= qq) & (kk >= jnp.maximum(qq - DB, lo))
        s = jnp.where(valid, s, -jnp.inf)
        m = jnp.max(s, axis=0, keepdims=True)
        p = jnp.exp2(s - m)
        l = jnp.sum(p, axis=0, keepdims=True)
        ot = _dot(vts[g, h * DH_D:(h + 1) * DH_D, pl.ds(r0, 2 * DB)], p.astype(BF16))
        yield
        yield ot / l, jnp.broadcast_to(m * (1.0 / LOG2E) + jnp.log(l), (DH_D, DB))

    for g, (_, d) in enumerate(DIL_GROUPS):
        def blocks(i, _, g=g, d=d):
            bis = [i * UB + t for t in range(UB)]
            gens = [head(g, h, bi, NB // d) for bi in bis for h in range(2)]
            for _ in range(2):
                for gen in gens:
                    next(gen)
            res = [next(gen) for gen in gens]
            for t, bi in enumerate(bis):
                (o0, l0), (o1, l1) = res[2 * t], res[2 * t + 1]
                start = nat_start(bi, d)
                rows = pl.ds(pl.multiple_of(start, DB), DB) if d == 1 else pl.ds(start, DB, stride=d)
                onat[g, rows, :] = jnp.concatenate([o0, o1], axis=0).T
                lnat[g, rows, :] = jnp.concatenate([l0, l1], axis=0).T
            return 0

        lax.fori_loop(0, NB // UB, blocks, 0)

    def merge(i, _):
        r0 = pl.multiple_of(i * DB, DB)
        ls = [lnat[g, pl.ds(r0, DB), :] for g in range(3)]
        mx = jnp.maximum(ls[0], jnp.maximum(ls[1], ls[2]))
        ws = [jnp.exp(l - mx) for l in ls]
        num = ws[0] * onat[0, pl.ds(r0, DB), :] + ws[1] * onat[1, pl.ds(r0, DB), :] + ws[2] * onat[2, pl.ds(r0, DB), :]
        o_ref[pl.ds(r0, DB), :] = (num / (ws[0] + ws[1] + ws[2])).astype(BF16)
        return 0

    lax.fori_loop(0, NB, merge, 0)


def mixer_d(pn, cos_a, sin_a):
    B = pn.shape[0] // SEQ
    S = SEQ
    ng = len(DIL_GROUPS)
    blk = lambda part, g: pl.BlockSpec((S, LANE), lambda b, p: (b, (PN_D + (part * ng + g) * BRANCH_W) // LANE + p))
    tab = pl.BlockSpec((S, LANE), lambda b, p: (b, 0))
    return pl.pallas_call(
        _mixd_kernel,
        grid=(B, H_D // 2),
        in_specs=[blk(part, g) for part in range(3) for g in range(ng)] + [tab, tab],
        out_specs=pl.BlockSpec((S, LANE), lambda b, p: (b, p)),
        out_shape=jax.ShapeDtypeStruct((B * S, BRANCH_W), BF16),
        scratch_shapes=[pltpu.VMEM((S, LANE), F32),
                        pltpu.VMEM((ng, 2, S, LANE), BF16),
                        pltpu.VMEM((ng, DB + S, LANE), BF16),
                        pltpu.VMEM((ng, LANE, DB + S), BF16),
                        pltpu.VMEM((ng, S, LANE), F32),
                        pltpu.VMEM((ng, S, LANE), F32)],
        compiler_params=_cparams(2),
        name="mixer_d",
    )(*([pn] * (3 * ng)), cos_a, sin_a)


GB = 128
GU = 2
GC = 256
NCHUNK = SEQ // CHUNK_B
PAD_B = 8


def _split3(x):
    hi = x.astype(BF16)
    r1 = x - hi.astype(F32)
    mid = r1.astype(BF16)
    lo = (r1 - mid.astype(F32)).astype(BF16)
    return hi, mid, lo


def _mixb_kernel(qkv_ref, misc_ref, conv_ref, alog_ref, dt_ref, ng_ref, o_ref,
                 xpad, gc_scr, gl_scr, gct_scr, qeff_scr, o0_scr, n_scr, g_scr, s_scr):
    S = SEQ
    RC = GC
    R = GB
    ri = lax.broadcasted_iota(jnp.int32, (R, R), 0)
    ci = lax.broadcasted_iota(jnp.int32, (R, R), 1)
    same = (ri // CHUNK_B) == (ci // CHUNK_B)
    tril = same & (ri >= ci)
    strict = same & (ri > ci)
    eye = jnp.where(ri == ci, 1.0, 0.0)

    rc = lax.broadcasted_iota(jnp.int32, (RC, RC), 0)
    cc = lax.broadcasted_iota(jnp.int32, (RC, RC), 1)
    same_c = (rc // CHUNK_B) == (cc // CHUNK_B)
    sel = jnp.concatenate([jnp.where(same_c & (rc >= cc), 1.0, 0.0), jnp.where(same_c, 1.0, 0.0)],
                          axis=0).astype(BF16)
    neg_a = -jnp.exp(alog_ref[...])
    xpad[:, 0:PAD_B, :] = jnp.zeros((xpad.shape[0], PAD_B, LANE), F32)

    def gprep(i, _):
        r0 = pl.multiple_of(i * RC, RC)
        for cb in range(xpad.shape[0]):
            xpad[cb, pl.ds(PAD_B + r0, RC), :] = qkv_ref[pl.ds(r0, RC), cb * LANE:(cb + 1) * LANE].astype(F32)
        x = misc_ref[pl.ds(r0, RC), :].astype(F32) + dt_ref[...]
        sp = jnp.maximum(x, 0.0) + jnp.log1p(jnp.exp(-jnp.abs(x)))
        g = neg_a * sp
        hi, mid, lo = _split3(g)
        acc = _dot(sel, jnp.concatenate([hi, mid, lo], axis=1))
        acc = acc[:, 0:LANE] + acc[:, LANE:2 * LANE] + acc[:, 2 * LANE:3 * LANE]
        gc_scr[pl.ds(r0, RC), :] = acc[:RC]
        gl_scr[pl.ds(r0, RC), :] = acc[RC:]
        gct_scr[i] = acc[:RC].T[0:8, :]
        return 0

    lax.fori_loop(0, S // RC, gprep, 0)

    def conv_cols(r0, cb, norm):
        cs = slice(cb * LANE, (cb + 1) * LANE)
        y = jnp.zeros((R, LANE), F32)
        for j in range(CONV_B):
            y = y + xpad[cb, pl.ds(r0 + (PAD_B - (CONV_B - 1) + j), R), :] * conv_ref[j:j + 1, cs]
        y = _silu(y)
        if norm:
            y = y * lax.rsqrt(jnp.sum(y * y, axis=-1, keepdims=True) + NORM_EPS)
        return y

    def group_head(gi, h):
        r0 = pl.multiple_of(gi * R, R)
        qn = conv_cols(r0, h, True)
        kn = conv_cols(r0, H_B + h, True)
        v = conv_cols(r0, 2 * H_B + h, False)
        mf = misc_ref[pl.ds(r0, R), :].astype(F32)
        beta = _sigmoid(mf[:, MISC_BETA + h:MISC_BETA + h + 1])
        gcol = gc_scr[pl.ds(r0, R), MISC_DECAY + h:MISC_DECAY + h + 1]
        glcol = gl_scr[pl.ds(r0, R), MISC_DECAY + h:MISC_DECAY + h + 1]
        grow = gct_scr[gi // (RC // R), MISC_DECAY + h:MISC_DECAY + h + 1,
                       pl.ds(pl.multiple_of((gi % (RC // R)) * R, R), R)]
        dec = jnp.exp(jnp.where(tril, gcol - grow, 0.0))
        kb = kn * beta
        knb = kn.astype(BF16)
        lm = jnp.where(strict, _nt(kb.astype(BF16), knb) * dec, 0.0)
        qkm = jnp.where(tril, _nt((qn * (DK_B ** -0.5)).astype(BF16), knb) * dec, 0.0)
        yield
        x = eye - jnp.where((ri // 2) == (ci // 2), lm, 0.0)
        s = 2
        while s < CHUNK_B:
            off = ((ri // (2 * s)) == (ci // (2 * s))) & ((ri // s) != (ci // s))
            xb = x.astype(BF16)
            t = _dot(jnp.where(off, lm, 0.0).astype(BF16), xb).astype(BF16)
            yield
            x = x - _dot(xb, t)
            yield
            s *= 2
        rhs = jnp.concatenate([v * beta, kb * jnp.exp(gcol)], axis=1).astype(BF16)
        uw = _dot(x.astype(BF16), rhs)
        yield
        uwb = uw.astype(BF16)
        ox = _dot(qkm.astype(BF16), uwb)
        yield
        qeff_scr[h, pl.ds(r0, R), :] = (qn * (DK_B ** -0.5) * jnp.exp(gcol) - ox[:, LANE:]).astype(BF16)
        o0_scr[h, pl.ds(r0, R), :] = ox[:, :LANE]
        kt = (kn * jnp.exp(glcol - gcol)).astype(BF16)
        for c in range(R // CHUNK_B):
            gn = _tn(kt[c * CHUNK_B:(c + 1) * CHUNK_B], uwb[c * CHUNK_B:(c + 1) * CHUNK_B])
            n_scr[h, gi * (R // CHUNK_B) + c] = gn[:, :LANE]
            g_scr[h, gi * (R // CHUNK_B) + c] = gn[:, LANE:].astype(BF16)

    def group(i, _):
        gens = [group_head(i * GU + u, h) for u in range(GU) for h in range(H_B)]
        while gens:
            gens = [g for g in gens if next(g, True) is None]
        return 0

    lax.fori_loop(0, S // (R * GU), group, 0)

    s_scr[...] = jnp.zeros(s_scr.shape, F32)

    def scan(c, _):
        r0 = pl.multiple_of(c * CHUNK_B, CHUNK_B)
        glrow = jnp.exp(gl_scr[pl.ds(r0, 1), :])
        for h in range(H_B):
            st = s_scr[h]
            sb = st.astype(BF16)
            o = _dot(qeff_scr[h, pl.ds(r0, CHUNK_B), :], sb) + o0_scr[h, pl.ds(r0, CHUNK_B), :]
            o = _rmsnorm_rows(o, ng_ref[...])
            o_ref[pl.ds(r0, CHUNK_B), h * LANE:(h + 1) * LANE] = o.astype(BF16)
            a = glrow[:, MISC_DECAY + h:MISC_DECAY + h + 1]
            s_scr[h] = a * st - _dot(g_scr[h, c], sb) + n_scr[h, c]
        return 0

    lax.fori_loop(0, NCHUNK, scan, 0, unroll=2)


def mixer_b(pn, conv_w, alog_row, dt_row, out_norm):
    B = pn.shape[0] // SEQ
    S = SEQ
    W = 3 * H_B * DK_B
    row = pl.BlockSpec((1, LANE), lambda b: (0, 0))
    return pl.pallas_call(
        _mixb_kernel,
        grid=(B,),
        in_specs=[pl.BlockSpec((S, W), lambda b: (b, PN_B // W), pipeline_mode=pl.Buffered(1)),
                  pl.BlockSpec((S, LANE), lambda b: (b, PN_MISC // LANE)),
                  pl.BlockSpec((CONV_B, W), lambda b: (0, 0)), row, row, row],
        out_specs=pl.BlockSpec((S, BRANCH_W), lambda b: (b, 0)),
        out_shape=jax.ShapeDtypeStruct((B * S, BRANCH_W), BF16),
        scratch_shapes=[pltpu.VMEM((W // LANE, PAD_B + S, LANE), F32),
                        pltpu.VMEM((S, LANE), F32),
                        pltpu.VMEM((S, LANE), F32),
                        pltpu.VMEM((S // GC, 8, GC), F32),
                        pltpu.VMEM((H_B, S, LANE), BF16),
                        pltpu.VMEM((H_B, S, LANE), F32),
                        pltpu.VMEM((H_B, NCHUNK, DK_B, DV_B), F32),
                        pltpu.VMEM((H_B, NCHUNK, DK_B, DK_B), BF16),
                        pltpu.VMEM((H_B, DK_B, DV_B), F32)],
        compiler_params=_cparams(1),
        name="mixer_b",
    )(pn, pn, conv_w, alog_row, dt_row, out_norm)


TM_MERGE = 512


def _merge_kernel(*refs):
    o_refs = refs[0:4]
    z_refs = refs[4:8]
    m_refs = refs[8:12]
    x_ref, mod_ref, wbr_ref, wout_ref, lng_ref, lnb_ref, out_ref = refs[12:]
    merged = jnp.zeros((TM_MERGE, D_MODEL), F32)
    for n in range(N_BRANCH):
        br = (o_refs[n][...].astype(F32) * _silu(z_refs[n][...].astype(F32))).astype(BF16)
        merged = merged + _dot(br, wbr_ref[n]) * _sigmoid(m_refs[n][...].astype(F32))
    y = _dot(merged.astype(BF16), wout_ref[...])
    t = DEEPNORM_ALPHA * x_ref[...] + mod_ref[0, 2:3, :] * y
    mu = jnp.mean(t, axis=-1, keepdims=True)
    tc = t - mu
    var = jnp.mean(tc * tc, axis=-1, keepdims=True)
    out_ref[...] = tc * lax.rsqrt(var + NORM_EPS) * lng_ref[...] + lnb_ref[...]


def merge_out(pn, o_a, o_b, o_c, o_d, x2, mod, w_br, w_out, ln_g, ln_b):
    T = x2.shape[0]
    tm = TM_MERGE
    per_b = SEQ // tm
    br = pl.BlockSpec((tm, BRANCH_W), lambda i: (i, 0))
    zspec = lambda n: pl.BlockSpec((tm, BRANCH_W), lambda i: (i, PN_Z // BRANCH_W + n))
    mspec = lambda n: pl.BlockSpec((tm, D_MODEL), lambda i: (i, PN_MERGE // D_MODEL + n))
    row = pl.BlockSpec((1, D_MODEL), lambda i: (0, 0))
    return pl.pallas_call(
        _merge_kernel,
        grid=(T // tm,),
        in_specs=[br, br, br, br] + [zspec(n) for n in range(4)] + [mspec(n) for n in range(4)] + [
            pl.BlockSpec((tm, D_MODEL), lambda i: (i, 0)),
            pl.BlockSpec((1, 3, D_MODEL), lambda i: (i // per_b, 0, 0)),
            pl.BlockSpec((N_BRANCH, BRANCH_W, D_MODEL), lambda i: (0, 0, 0)),
            pl.BlockSpec((D_MODEL, D_MODEL), lambda i: (0, 0)), row, row],
        out_specs=pl.BlockSpec((tm, D_MODEL), lambda i: (i, 0)),
        out_shape=jax.ShapeDtypeStruct((T, D_MODEL), F32),
        compiler_params=_cparams(1),
        name="merge_out",
    )(o_a, o_b, o_c, o_d, pn, pn, pn, pn, pn, pn, pn, pn, x2, mod, w_br, w_out, ln_g, ln_b)


def _prep_w_in(w):
    zc = lambda n: jnp.zeros((D_MODEL, n), w.dtype)
    misc = jnp.concatenate([w[:, _OFF_BETA:_OFF_BETA + 4], w[:, _OFF_DECAY:_OFF_DECAY + 4], zc(MISC_KPE - 8),
                            w[:, _OFF_KPE:_OFF_KPE + ROPE_C]], axis=1)
    wn = jnp.concatenate([w[:, _OFF_A:_OFF_A + 1536], w[:, _OFF_BQKV:_OFF_BQKV + 1536],
                          w[:, _OFF_CQ:_OFF_CQ + Q_LORA_C], w[:, _OFF_CKV:_OFF_CKV + KV_LORA_C], misc,
                          w[:, _OFF_D:_OFF_Z], w[:, _OFF_Z:_OFF_Z + 2048], w[:, _OFF_MERGE:_OFF_MERGE + 4096]],
                         axis=1)
    return wn.astype(BF16)


def _prep_w_c(w_uq, w_ukv):
    q = w_uq.reshape(Q_LORA_C, H_C, NOPE_C + ROPE_C)
    wq = jnp.concatenate([q[..., :NOPE_C], jnp.zeros((Q_LORA_C, H_C, LANE - NOPE_C - ROPE_C), q.dtype),
                          q[..., NOPE_C:]], axis=-1).reshape(Q_LORA_C, H_C * LANE)
    kv = w_ukv.reshape(KV_LORA_C, H_C, NOPE_C + V_C)
    wk = jnp.concatenate([kv[..., :NOPE_C], jnp.zeros((KV_LORA_C, H_C, LANE - NOPE_C), kv.dtype)],
                         axis=-1).reshape(KV_LORA_C, H_C * LANE)
    wvt = kv[..., NOPE_C:].reshape(KV_LORA_C, H_C * V_C).T
    return wq.astype(BF16), wk.astype(BF16), wvt.astype(BF16)


def _lane_row(vals, at):
    return jnp.zeros((1, LANE), F32).at[0, at:at + vals.shape[0]].set(vals.astype(F32))


def kernel(x, c, positions, w_ada, b_ada, w_in, conv_b, a_log, dt_bias, out_norm_b, lambda_q1, lambda_k1,
           lambda_q2, lambda_k2, subln_g, q_norm_c, w_uq, kv_norm_c, w_ukv, w_br, w_out, ln_g, ln_b):
    B, S, D = x.shape
    assert (S, D) == (SEQ, D_MODEL) and w_in.shape[0] == DEPTH
    T = B * S
    mods = ada_modulation(c, w_ada, b_ada)
    cos_a, sin_a, cos_c, sin_c = rope_tables(positions)
    x2 = x.reshape(T, D)
    for l in range(DEPTH):
        mod = mods[l]
        pn = input_projection(x2, mod, _prep_w_in(w_in[l]))
        lam = jnp.stack([lambda_q1[l], lambda_k1[l], lambda_q2[l], lambda_k2[l]]).astype(F32)
        o_a = mixer_a(pn, cos_a, sin_a, lam, subln_g[l].reshape(LANE, 1),
                      lam_init=0.8 - 0.6 * math.exp(-0.3 * l))
        o_b = mixer_b(pn, conv_b[l], _lane_row(a_log[l], MISC_DECAY), _lane_row(dt_bias[l], MISC_DECAY),
                      out_norm_b[l].reshape(1, LANE))
        wq, wk, wv = _prep_w_c(w_uq[l], w_ukv[l])
        o_c = mixer_c(pn, cos_c, sin_c, q_norm_c[l].reshape(1, Q_LORA_C), kv_norm_c[l].reshape(1, KV_LORA_C),
                      wq, wk, wv)
        o_d = mixer_d(pn, cos_a, sin_a)
        x2 = merge_out(pn, o_a, o_b, o_c, o_d, x2, mod, w_br[l].astype(BF16), w_out[l].astype(BF16),
                       ln_g[l].reshape(1, D), ln_b[l].reshape(1, D))
    return x2.reshape(B, S, D)
```

```python
import functools
import math

import jax
import jax.numpy as jnp
from jax import lax
from jax.experimental import pallas as pl
from jax.experimental.pallas import tpu as pltpu

F32 = jnp.float32
BF16 = jnp.bfloat16

D_MODEL = 1024
SEQ = 2048
DEPTH = 2
ROPE_THETA = 500000.0
NORM_EPS = 1e-6
H_A, DH_A = 4, 64
H_B, DK_B, DV_B, CONV_B, CHUNK_B = 4, 128, 128, 4, 64
H_C, Q_LORA_C, KV_LORA_C, NOPE_C, ROPE_C, V_C = 8, 256, 128, 64, 32, 64
H_D, DH_D = 8, 64
DIL_GROUPS = ((128, 1), (512, 4), (2048, 16))
N_BRANCH, BRANCH_W = 4, 512
DEEPNORM_ALPHA = (2.0 * DEPTH) ** 0.25

LANE = 128
VMEM_LIMIT = 56 * 1024 * 1024

_OFF_A = 0
_OFF_BQKV = 1536
_OFF_BETA = 3072
_OFF_DECAY = 3076
_OFF_CQ = 3080
_OFF_CKV = 3336
_OFF_KPE = 3464
_OFF_D = 3496
_OFF_Z = 8104
_OFF_MERGE = 10152
D_IN = 14248

PN_A = 0
PN_B = 1536
PN_CQ = 3072
PN_CKV = 3328
PN_MISC = 3456
PN_D = 3584
PN_Z = 8192
PN_MERGE = 10240
PN_W = 14336
MISC_BETA, MISC_DECAY, MISC_KPE = 0, 4, 96


def _cparams(n_grid):
    return pltpu.CompilerParams(dimension_semantics=("arbitrary",) * n_grid,
                                vmem_limit_bytes=VMEM_LIMIT)


def _nt(a, b):
    return lax.dot_general(a, b, (((1,), (1,)), ((), ())), preferred_element_type=F32)


def _tn(a, b):
    return lax.dot_general(a, b, (((0,), (0,)), ((), ())), preferred_element_type=F32)


def _dot(a, b):
    return jnp.dot(a, b, preferred_element_type=F32)


def _sigmoid(x):
    return 1.0 / (1.0 + jnp.exp(-x))


def _sigmoid_tanh(x):
    return 0.5 * jnp.tanh(0.5 * x) + 0.5


def _silu(x):
    return x * _sigmoid(x)


def _ada_kernel(c_ref, w_ref, b_ref, o_ref):
    ca = _silu(c_ref[...]).astype(BF16)
    o_ref[0] = _dot(ca, w_ref[0].astype(BF16)) + b_ref[0]


def ada_modulation(c, w_ada, b_ada):
    B = c.shape[0]
    L = w_ada.shape[0]
    out = pl.pallas_call(
        _ada_kernel,
        grid=(L, 3),
        in_specs=[pl.BlockSpec((B, D_MODEL), lambda l, j: (0, 0)),
                  pl.BlockSpec((1, D_MODEL, D_MODEL), lambda l, j: (l, 0, j)),
                  pl.BlockSpec((1, 1, D_MODEL), lambda l, j: (l, 0, j))],
        out_specs=pl.BlockSpec((1, B, D_MODEL), lambda l, j: (l, 0, j)),
        out_shape=jax.ShapeDtypeStruct((L, B, 3 * D_MODEL), F32),
        compiler_params=_cparams(2),
        name="ada_modulation",
    )(c, w_ada, b_ada.reshape(L, 1, 3 * D_MODEL))
    return out.reshape(L, B, 3, D_MODEL)


def _rope_rows():
    half_a = (DH_A // 4) // 2
    inv_a = ROPE_THETA ** (-jnp.arange(half_a, dtype=F32) / half_a)
    half_c = ROPE_C // 2
    inv_c = ROPE_THETA ** (-jnp.arange(half_c, dtype=F32) / half_c)
    z = jnp.zeros
    inv = jnp.concatenate([inv_a, inv_a, z(48, F32), inv_a, inv_a, z(16, F32), inv_c, inv_c])
    sgn = jnp.concatenate([-jnp.ones(8, F32), jnp.ones(8, F32), z(48, F32),
                           -jnp.ones(8, F32), jnp.ones(8, F32), z(16, F32),
                           -jnp.ones(16, F32), jnp.ones(16, F32)])
    return inv.reshape(1, LANE), sgn.reshape(1, LANE)


def _rope_table_kernel(pos_ref, inv_ref, sgn_ref, cos_a, sin_a, cos_c, sin_c):
    R = 256

    def body(i, _):
        r0 = pl.multiple_of(i * R, R)
        ang = pos_ref[pl.ds(r0, R), :].astype(F32) * inv_ref[...]
        cv = jnp.cos(ang)
        sv = jnp.sin(ang) * sgn_ref[...]
        lane = lax.broadcasted_iota(jnp.int32, (R, LANE), 1)
        is_c = lane >= MISC_KPE
        cos_a[pl.ds(r0, R), :] = jnp.where(is_c, 1.0, cv)
        sin_a[pl.ds(r0, R), :] = jnp.where(is_c, 0.0, sv)
        cos_c[pl.ds(r0, R), :] = jnp.where(is_c, cv, 1.0)
        sin_c[pl.ds(r0, R), :] = jnp.where(is_c, sv, 0.0)
        return 0

    lax.fori_loop(0, SEQ // R, body, 0)


def rope_tables(positions):
    B, S = positions.shape
    inv, sgn = _rope_rows()
    tab = jax.ShapeDtypeStruct((B * S, LANE), F32)
    row = pl.BlockSpec((1, LANE), lambda b: (0, 0))
    blk = pl.BlockSpec((S, LANE), lambda b: (b, 0))
    return pl.pallas_call(
        _rope_table_kernel,
        grid=(B,),
        in_specs=[pl.BlockSpec((S, 1), lambda b: (b, 0)), row, row],
        out_specs=[blk, blk, blk, blk],
        out_shape=[tab, tab, tab, tab],
        compiler_params=_cparams(1),
        name="rope_tables",
    )(positions.reshape(B * S, 1), inv, sgn)


TM_IN = 1024
TN_IN = 2048


def _inproj_kernel(x_ref, mod_ref, w_ref, o_ref, h_ref):
    @pl.when(pl.program_id(1) == 0)
    def _():
        shift = mod_ref[0, 0:1, :]
        scale1 = 1.0 + mod_ref[0, 1:2, :]
        R = 128

        def body(i, _):
            r0 = pl.multiple_of(i * R, R)
            xs = x_ref[pl.ds(r0, R), :]
            mu = jnp.mean(xs, axis=-1, keepdims=True)
            xc = xs - mu
            var = jnp.mean(xc * xc, axis=-1, keepdims=True)
            hn = xc * lax.rsqrt(var + NORM_EPS)
            h_ref[pl.ds(r0, R), :] = (hn * scale1 + shift).astype(BF16)
            return 0

        lax.fori_loop(0, TM_IN // R, body, 0)

    o_ref[...] = _dot(h_ref[...], w_ref[...]).astype(BF16)


def input_projection(x2, mod, w):
    T = x2.shape[0]
    ncol = w.shape[1]
    return pl.pallas_call(
        _inproj_kernel,
        grid=(T // TM_IN, ncol // TN_IN),
        in_specs=[pl.BlockSpec((TM_IN, D_MODEL), lambda i, j: (i, 0)),
                  pl.BlockSpec((1, 3, D_MODEL), lambda i, j: (i // (SEQ // TM_IN), 0, 0)),
                  pl.BlockSpec((D_MODEL, TN_IN), lambda i, j: (0, j))],
        out_specs=pl.BlockSpec((TM_IN, TN_IN), lambda i, j: (i, j)),
        out_shape=jax.ShapeDtypeStruct((T, ncol), BF16),
        scratch_shapes=[pltpu.VMEM((TM_IN, D_MODEL), BF16)],
        compiler_params=_cparams(2),
        name="input_projection",
    )(x2, mod, w)


def _rope64(x, cos, sin, lane):
    sw = jnp.where((lane & 63) < 8, pltpu.roll(x, LANE - 8, 1), pltpu.roll(x, 8, 1))
    return x * cos + sw * sin


TQ = 256
TKB = 256
LOG2E = 1.4426950408889634


def _attention_pipeline(nq, nsub, operands, finish):
    items = [(qi, sub) for qi in range(nq) for sub in range(nsub)]
    state = {}

    def phase_a(it):
        qi, _ = it
        k_ref, q_blk, _, s_ref, _ = operands(*it)
        kmax = (qi + 1) * TQ
        m = None
        for c0 in range(0, kmax, 2 * TKB):
            c1 = min(c0 + 2 * TKB, kmax)
            s = _nt(k_ref[c0:c1, :], q_blk)
            if c1 == kmax:
                kk = lax.broadcasted_iota(jnp.int32, (c1 - c0, TQ), 0) + (c0 - qi * TQ)
                qq = lax.broadcasted_iota(jnp.int32, (c1 - c0, TQ), 1)
                s = jnp.where(kk <= qq, s, -jnp.inf)
            s_ref[c0:c1, :] = s
            mc = jnp.max(s, axis=0, keepdims=True)
            m = mc if m is None else jnp.maximum(m, mc)
            yield
        state[it] = {"m": m}

    def phase_b(it):
        qi, _ = it
        _, _, _, s_ref, p_ref = operands(*it)
        m = state[it]["m"]
        l = None
        for t in range(qi + 1):
            p = jnp.exp2(s_ref[t * TKB:(t + 1) * TKB, :] - m)
            p_ref[t * TKB:(t + 1) * TKB, :] = p.astype(BF16)
            lt = jnp.sum(p, axis=0, keepdims=True)
            l = lt if l is None else l + lt
            yield
        state[it]["l"] = l

    def phase_c(it):
        qi, _ = it
        _, _, vt_ref, _, p_ref = operands(*it)
        kmax = (qi + 1) * TQ
        acc = None
        for c0 in range(0, kmax, 2 * TKB):
            c1 = min(c0 + 2 * TKB, kmax)
            d = _dot(vt_ref[:, c0:c1], p_ref[c0:c1, :])
            acc = d if acc is None else acc + d
            yield
        state[it]["o"] = acc / state[it]["l"]

    n = len(items)
    for step in range(n + 2):
        gens = []
        if step < n:
            gens.append(phase_a(items[step]))
        if 0 <= step - 1 < n:
            gens.append(phase_b(items[step - 1]))
        if 0 <= step - 2 < n:
            gens.append(phase_c(items[step - 2]))
        while gens:
            gens = [g for g in gens if next(g, True) is None]
        done = step - 2
        if done >= 0 and items[done][1] == nsub - 1:
            qi = items[done][0]
            finish(qi, [state.pop((qi, sub))["o"] for sub in range(nsub)])


def _mixa_kernel(q_ref, k_ref, v_ref, cos_ref, sin_ref, lam_ref, g_ref, o_ref, q2_ref, kr_ref, vt_ref, s_ref, p_ref,
                 *, lam_init):
    S = SEQ
    R = 256

    def prep(i, _):
        r0 = pl.multiple_of(i * R, R)
        lane = lax.broadcasted_iota(jnp.int32, (R, LANE), 1)
        cos = cos_ref[pl.ds(r0, R), :]
        sin = sin_ref[pl.ds(r0, R), :]
        q = _rope64(q_ref[pl.ds(r0, R), :].astype(F32), cos, sin, lane) * (DH_A ** -0.5 * LOG2E)
        q2_ref[0, pl.ds(r0, R), :] = jnp.where(lane < 64, q, 0.0).astype(BF16)
        q2_ref[1, pl.ds(r0, R), :] = jnp.where(lane >= 64, q, 0.0).astype(BF16)
        kr_ref[pl.ds(r0, R), :] = _rope64(k_ref[pl.ds(r0, R), :].astype(F32), cos, sin, lane).astype(BF16)
        vt_ref[:, pl.ds(r0, R)] = v_ref[pl.ds(r0, R), :].astype(F32).T.astype(BF16)
        return 0

    lax.fori_loop(0, S // R, prep, 0)

    lp = lam_ref[...]
    lam = (jnp.exp(jnp.sum(lp[0:1] * lp[1:2], axis=-1, keepdims=True))
           - jnp.exp(jnp.sum(lp[2:3] * lp[3:4], axis=-1, keepdims=True)) + lam_init)
    gcol = g_ref[...] * (1.0 - lam_init)

    def finish(qi, outs):
        o = outs[0] - lam * outs[1]
        ms = jnp.mean(o * o, axis=0, keepdims=True)
        o = o * lax.rsqrt(ms + NORM_EPS) * gcol
        o_ref[qi * TQ:(qi + 1) * TQ, :] = o.T.astype(BF16)

    _attention_pipeline(S // TQ, 2,
                        lambda qi, mp: (kr_ref, q2_ref[mp, qi * TQ:(qi + 1) * TQ, :], vt_ref, s_ref.at[mp], p_ref.at[mp]),
                        finish)


def mixer_a(pn, cos_a, sin_a, lam_params, subln_g_col, *, lam_init=0.2):
    B = pn.shape[0] // SEQ
    S = SEQ
    col = lambda off: (lambda b, h: (b, off // LANE + h))
    blk = lambda off: pl.BlockSpec((S, LANE), col(off))
    tab = pl.BlockSpec((S, LANE), lambda b, h: (b, 0))
    return pl.pallas_call(
        functools.partial(_mixa_kernel, lam_init=lam_init),
        grid=(B, H_A),
        in_specs=[blk(PN_A), blk(PN_A + 512), blk(PN_A + 1024), tab, tab,
                  pl.BlockSpec((4, DH_A), lambda b, h: (0, 0)),
                  pl.BlockSpec((LANE, 1), lambda b, h: (0, 0))],
        out_specs=pl.BlockSpec((S, LANE), lambda b, h: (b, h)),
        out_shape=jax.ShapeDtypeStruct((B * S, BRANCH_W), BF16),
        scratch_shapes=[pltpu.VMEM((2, S, LANE), BF16), pltpu.VMEM((S, LANE), BF16), pltpu.VMEM((LANE, S), BF16),
                        pltpu.VMEM((2, S, TQ), F32), pltpu.VMEM((2, S, TQ), BF16)],
        compiler_params=_cparams(2),
        name="mixer_a",
    )(pn, pn, pn, cos_a, sin_a, lam_params, subln_g_col)


def _rmsnorm_rows(x, g):
    return x * lax.rsqrt(jnp.mean(x * x, axis=-1, keepdims=True) + NORM_EPS) * g


def _rope_c(x, cos, sin, lane):
    sw = jnp.where(lane < 112, pltpu.roll(x, LANE - 16, 1), pltpu.roll(x, 16, 1))
    return x * cos + sw * sin


def _mixc_kernel(cq_ref, ckv_ref, misc_ref, cos_ref, sin_ref, qg_ref, kvg_ref, wuq_ref, wuqs_ref, wuk_ref, wuvt_ref,
                 o_ref, q_scr, k_scr, vt_scr, s_ref, p_ref):
    S = SEQ
    R = 256
    scale = (NOPE_C + ROPE_C) ** -0.5 * LOG2E

    def prep(i, _):
        r0 = pl.multiple_of(i * R, R)
        lane = lax.broadcasted_iota(jnp.int32, (R, LANE), 1)
        cos = cos_ref[pl.ds(r0, R), :]
        sin = sin_ref[pl.ds(r0, R), :]
        cqn = _rmsnorm_rows(cq_ref[pl.ds(r0, R), :].astype(F32), qg_ref[...]).astype(BF16)
        q = _dot(cqn, wuq_ref[...])
        qsw = _dot(cqn, wuqs_ref[...])
        kvn = _rmsnorm_rows(ckv_ref[pl.ds(r0, R), :].astype(F32), kvg_ref[...]).astype(BF16)
        kk = _dot(kvn, wuk_ref[...])
        kpe = jnp.where(lane >= MISC_KPE, misc_ref[pl.ds(r0, R), :].astype(F32), 0.0)
        kpe = _rope_c(kpe, cos, sin, lane)
        for e in range(2):
            qe = (q[:, e * LANE:(e + 1) * LANE] * cos + qsw[:, e * LANE:(e + 1) * LANE] * sin) * scale
            q_scr[e, pl.ds(r0, R), :] = qe.astype(BF16)
            k_scr[e, pl.ds(r0, R), :] = (kk[:, e * LANE:(e + 1) * LANE] + kpe).astype(BF16)
        vt_scr[:, pl.ds(r0, R)] = _nt(wuvt_ref[...], kvn).astype(BF16)
        return 0

    lax.fori_loop(0, S // R, prep, 0)

    def finish(qi, outs):
        o_ref[qi * TQ:(qi + 1) * TQ, :] = jnp.concatenate(outs, axis=0).T.astype(BF16)

    _attention_pipeline(S // TQ, 2,
                        lambda qi, e: (k_scr.at[e], q_scr[e, qi * TQ:(qi + 1) * TQ, :],
                                       vt_scr.at[e * V_C:(e + 1) * V_C], s_ref.at[e], p_ref.at[e]),
                        finish)


def mixer_c(pn, cos_c, sin_c, q_norm, kv_norm, wuq, wuqs, wuk, wuvt):
    B = pn.shape[0] // SEQ
    S = SEQ
    tab = pl.BlockSpec((S, LANE), lambda b, p: (b, 0))
    return pl.pallas_call(
        _mixc_kernel,
        grid=(B, H_C // 2),
        in_specs=[pl.BlockSpec((S, Q_LORA_C), lambda b, p: (b, PN_CQ // Q_LORA_C)),
                  pl.BlockSpec((S, LANE), lambda b, p: (b, PN_CKV // LANE)),
                  pl.BlockSpec((S, LANE), lambda b, p: (b, PN_MISC // LANE)),
                  tab, tab,
                  pl.BlockSpec((1, Q_LORA_C), lambda b, p: (0, 0)),
                  pl.BlockSpec((1, KV_LORA_C), lambda b, p: (0, 0)),
                  pl.BlockSpec((Q_LORA_C, 2 * LANE), lambda b, p: (0, p)),
                  pl.BlockSpec((Q_LORA_C, 2 * LANE), lambda b, p: (0, p)),
                  pl.BlockSpec((KV_LORA_C, 2 * LANE), lambda b, p: (0, p)),
                  pl.BlockSpec((LANE, KV_LORA_C), lambda b, p: (p, 0))],
        out_specs=pl.BlockSpec((S, LANE), lambda b, p: (b, p)),
        out_shape=jax.ShapeDtypeStruct((B * S, BRANCH_W), BF16),
        scratch_shapes=[pltpu.VMEM((2, S, LANE), BF16), pltpu.VMEM((2, S, LANE), BF16), pltpu.VMEM((LANE, S), BF16),
                        pltpu.VMEM((2, S, TQ), F32), pltpu.VMEM((2, S, TQ), BF16)],
        compiler_params=_cparams(2),
        name="mixer_c",
    )(pn, pn, pn, cos_c, sin_c, q_norm, kv_norm, wuq, wuqs, wuk, wuvt)


DB = 128
LSE_W = 16


def _mixd_kernel(q0_ref, q1_ref, q2_ref, k0_ref, k1_ref, k2_ref, v0_ref, v1_ref, v2_ref, cos_ref, sin_ref, o_ref,
                 nat, qs, ks, vts, onat, lnat):
    S = SEQ
    NB = S // DB
    UB = 4
    UA = 8
    RB = UB * DB
    lane = lax.broadcasted_iota(jnp.int32, (RB, LANE), 1)
    lane_b = lax.broadcasted_iota(jnp.int32, (DB, LANE), 1)
    q_refs = (q0_ref, q1_ref, q2_ref)
    k_refs = (k0_ref, k1_ref, k2_ref)
    v_refs = (v0_ref, v1_ref, v2_ref)

    def nat_start(bi, d):
        nblk = NB // d
        return (bi % nblk) * (DB * d) + bi // nblk

    for g in range(len(DIL_GROUPS)):
        ks[g, 0:DB, :] = jnp.zeros((DB, LANE), BF16)
        vts[g, :, 0:DB] = jnp.zeros((LANE, DB), BF16)

    for g, (_, d) in enumerate(DIL_GROUPS):
        def sources(r0, g=g):
            cos = cos_ref[pl.ds(r0, RB), :]
            sin = sin_ref[pl.ds(r0, RB), :]
            q = _rope64(q_refs[g][pl.ds(r0, RB), :].astype(F32), cos, sin, lane) * (DH_D ** -0.5 * LOG2E)
            k = _rope64(k_refs[g][pl.ds(r0, RB), :].astype(F32), cos, sin, lane)
            return q, k, v_refs[g][pl.ds(r0, RB), :].astype(F32)

        def sink(bi, q, k, v, g=g):
            r0 = pl.multiple_of(bi * DB, DB)
            qs[g, 0, pl.ds(r0, DB), :] = jnp.where(lane_b < 64, q, 0.0).astype(BF16)
            qs[g, 1, pl.ds(r0, DB), :] = jnp.where(lane_b >= 64, q, 0.0).astype(BF16)
            ks[g, pl.ds(pl.multiple_of(DB + r0, DB), DB), :] = k.astype(BF16)
            vts[g, :, pl.ds(pl.multiple_of(DB + r0, DB), DB)] = v.T.astype(BF16)

        if d == 1:
            def direct(i, _):
                q, k, v = sources(pl.multiple_of(i * RB, RB))
                for t in range(UB):
                    rows = slice(t * DB, (t + 1) * DB)
                    sink(i * UB + t, q[rows], k[rows], v[rows])
                return 0
            lax.fori_loop(0, NB // UB, direct, 0)
        else:
            def stage(i, _):
                r0 = pl.multiple_of(i * RB, RB)
                for t, x in enumerate(sources(r0)):
                    nat[t, pl.ds(r0, RB), :] = x
                return 0
            lax.fori_loop(0, NB // UB, stage, 0)

            def gather(i, _, d=d):
                for t in range(UB):
                    bi = i * UB + t
                    rows = pl.ds(nat_start(bi, d), DB, stride=d)
                    sink(bi, nat[0, rows, :], nat[1, rows, :], nat[2, rows, :])
                return 0
            lax.fori_loop(0, NB // UB, gather, 0)

    def head(g, h, bi, nblk):
        r0 = pl.multiple_of(bi * DB, DB)
        if nblk == 1:
            nk, k0 = DB, pl.multiple_of(r0 + DB, DB)
            kk = lax.broadcasted_iota(jnp.int32, (DB, DB), 0)
            valid = kk <= lax.broadcasted_iota(jnp.int32, (DB, DB), 1)
        else:
            nk, k0 = 2 * DB, r0
            kk = lax.broadcasted_iota(jnp.int32, (2 * DB, DB), 0)
            qq = lax.broadcasted_iota(jnp.int32, (2 * DB, DB), 1) + DB
            lo = jnp.where(bi % nblk == 0, DB, 0)
            valid = (kk <= qq) & (kk >= jnp.maximum(qq - DB, lo))
        s = _nt(ks[g, pl.ds(k0, nk), :], qs[g, h, pl.ds(r0, DB), :])
        yield
        s = jnp.where(valid, s, -jnp.inf)
        m = jnp.max(s, axis=0, keepdims=True)
        p = jnp.exp2(s - m)
        l = jnp.sum(p, axis=0, keepdims=True)
        ot = _dot(vts[g, h * DH_D:(h + 1) * DH_D, pl.ds(k0, nk)], p.astype(BF16))
        yield
        yield ot / l, jnp.broadcast_to(m * (1.0 / LOG2E) + jnp.log(l), (DH_D, DB))

    for g, (_, d) in enumerate(DIL_GROUPS):
        def blocks(i, _, g=g, d=d):
            bis = [i * UA + t for t in range(UA)]
            gens = [head(g, h, bi, NB // d) for bi in bis for h in range(2)]
            for _ in range(2):
                for gen in gens:
                    next(gen)
            res = [next(gen) for gen in gens]
            for t, bi in enumerate(bis):
                (o0, l0), (o1, l1) = res[2 * t], res[2 * t + 1]
                start = nat_start(bi, d)
                rows = pl.ds(pl.multiple_of(start, DB), DB) if d == 1 else pl.ds(start, DB, stride=d)
                onat[g, rows, :] = jnp.concatenate([o0, o1], axis=0).T
                lnat[g, rows, :] = jnp.concatenate([l0, l1], axis=0).T
            return 0

        lax.fori_loop(0, NB // UA, blocks, 0)

    def merge(i, _):
        r0 = pl.multiple_of(i * DB, DB)
        ls = [lnat[g, pl.ds(r0, DB), :] for g in range(3)]
        mx = jnp.maximum(ls[0], jnp.maximum(ls[1], ls[2]))
        ws = [jnp.exp(l - mx) for l in ls]
        num = ws[0] * onat[0, pl.ds(r0, DB), :] + ws[1] * onat[1, pl.ds(r0, DB), :] + ws[2] * onat[2, pl.ds(r0, DB), :]
        o_ref[pl.ds(r0, DB), :] = (num / (ws[0] + ws[1] + ws[2])).astype(BF16)
        return 0

    lax.fori_loop(0, NB, merge, 0)


def mixer_d(pn, cos_a, sin_a):
    B = pn.shape[0] // SEQ
    S = SEQ
    ng = len(DIL_GROUPS)
    blk = lambda part, g: pl.BlockSpec((S, LANE), lambda b, p: (b, (PN_D + (part * ng + g) * BRANCH_W) // LANE + p))
    tab = pl.BlockSpec((S, LANE), lambda b, p: (b, 0))
    return pl.pallas_call(
        _mixd_kernel,
        grid=(B, H_D // 2),
        in_specs=[blk(part, g) for part in range(3) for g in range(ng)] + [tab, tab],
        out_specs=pl.BlockSpec((S, LANE), lambda b, p: (b, p)),
        out_shape=jax.ShapeDtypeStruct((B * S, BRANCH_W), BF16),
        scratch_shapes=[pltpu.VMEM((3, S, LANE), F32),
                        pltpu.VMEM((ng, 2, S, LANE), BF16),
                        pltpu.VMEM((ng, DB + S, LANE), BF16),
                        pltpu.VMEM((ng, LANE, DB + S), BF16),
                        pltpu.VMEM((ng, S, LANE), F32),
                        pltpu.VMEM((ng, S, LANE), F32)],
        compiler_params=_cparams(2),
        name="mixer_d",
    )(*([pn] * (3 * ng)), cos_a, sin_a)


GB = 128
GU = 2
GC = 256
NCHUNK = SEQ // CHUNK_B
PAD_B = 8


def _split3(x):
    hi = x.astype(BF16)
    r1 = x - hi.astype(F32)
    mid = r1.astype(BF16)
    lo = (r1 - mid.astype(F32)).astype(BF16)
    return hi, mid, lo


def _mixb_kernel(qkv_ref, misc_ref, conv_ref, alog_ref, dt_ref, ng_ref, o_ref,
                 xpad, gc_scr, gl_scr, gct_scr, qeff_scr, o0_scr, n_scr, g_scr, s_scr):
    S = SEQ
    RC = GC
    R = GB
    ri = lax.broadcasted_iota(jnp.int32, (R, R), 0)
    ci = lax.broadcasted_iota(jnp.int32, (R, R), 1)
    same = (ri // CHUNK_B) == (ci // CHUNK_B)
    tril = same & (ri >= ci)
    strict = same & (ri > ci)
    eye = jnp.where(ri == ci, 1.0, 0.0)

    rc = lax.broadcasted_iota(jnp.int32, (RC, RC), 0)
    cc = lax.broadcasted_iota(jnp.int32, (RC, RC), 1)
    same_c = (rc // CHUNK_B) == (cc // CHUNK_B)
    sel = jnp.concatenate([jnp.where(same_c & (rc >= cc), 1.0, 0.0), jnp.where(same_c, 1.0, 0.0)],
                          axis=0).astype(BF16)
    neg_a = -jnp.exp(alog_ref[...])
    xpad[:, 0:PAD_B, :] = jnp.zeros((xpad.shape[0], PAD_B, LANE), F32)

    def gprep(i, _):
        r0 = pl.multiple_of(i * RC, RC)
        for cb in range(xpad.shape[0]):
            xpad[cb, pl.ds(PAD_B + r0, RC), :] = qkv_ref[pl.ds(r0, RC), cb * LANE:(cb + 1) * LANE].astype(F32)
        x = misc_ref[pl.ds(r0, RC), :].astype(F32) + dt_ref[...]
        sp = jnp.maximum(x, 0.0) + jnp.log1p(jnp.exp(-jnp.abs(x)))
        g = neg_a * sp
        hi, mid, lo = _split3(g)
        acc = _dot(sel, jnp.concatenate([hi, mid, lo], axis=1))
        acc = acc[:, 0:LANE] + acc[:, LANE:2 * LANE] + acc[:, 2 * LANE:3 * LANE]
        gc_scr[pl.ds(r0, RC), :] = acc[:RC]
        gl_scr[pl.ds(r0, RC), :] = acc[RC:]
        gct_scr[i] = acc[:RC].T[0:8, :]
        return 0

    lax.fori_loop(0, S // RC, gprep, 0)

    def conv_cols(r0, cb, norm):
        cs = slice(cb * LANE, (cb + 1) * LANE)
        y = jnp.zeros((R, LANE), F32)
        for j in range(CONV_B):
            y = y + xpad[cb, pl.ds(r0 + (PAD_B - (CONV_B - 1) + j), R), :] * conv_ref[j:j + 1, cs]
        y = _silu(y)
        if norm:
            y = y * lax.rsqrt(jnp.sum(y * y, axis=-1, keepdims=True) + NORM_EPS)
        return y

    def group_head(gi, h):
        r0 = pl.multiple_of(gi * R, R)
        qn = conv_cols(r0, h, True)
        kn = conv_cols(r0, H_B + h, True)
        v = conv_cols(r0, 2 * H_B + h, False)
        mf = misc_ref[pl.ds(r0, R), :].astype(F32)
        beta = _sigmoid(mf[:, MISC_BETA + h:MISC_BETA + h + 1])
        gcol = gc_scr[pl.ds(r0, R), MISC_DECAY + h:MISC_DECAY + h + 1]
        glcol = gl_scr[pl.ds(r0, R), MISC_DECAY + h:MISC_DECAY + h + 1]
        grow = gct_scr[gi // (RC // R), MISC_DECAY + h:MISC_DECAY + h + 1,
                       pl.ds(pl.multiple_of((gi % (RC // R)) * R, R), R)]
        dec = jnp.exp(jnp.where(tril, gcol - grow, 0.0))
        kb = kn * beta
        knb = kn.astype(BF16)
        lm = jnp.where(strict, _nt(kb.astype(BF16), knb) * dec, 0.0)
        qkm = jnp.where(tril, _nt((qn * (DK_B ** -0.5)).astype(BF16), knb) * dec, 0.0)
        yield
        x = eye - jnp.where((ri // 2) == (ci // 2), lm, 0.0)
        s = 2
        while s < CHUNK_B:
            off = ((ri // (2 * s)) == (ci // (2 * s))) & ((ri // s) != (ci // s))
            xb = x.astype(BF16)
            t = _dot(jnp.where(off, lm, 0.0).astype(BF16), xb).astype(BF16)
            yield
            x = x - _dot(xb, t)
            yield
            s *= 2
        rhs = jnp.concatenate([v * beta, kb * jnp.exp(gcol)], axis=1).astype(BF16)
        uw = _dot(x.astype(BF16), rhs)
        yield
        uwb = uw.astype(BF16)
        ox = _dot(qkm.astype(BF16), uwb)
        yield
        qeff_scr[h, pl.ds(r0, R), :] = (qn * (DK_B ** -0.5) * jnp.exp(gcol) - ox[:, LANE:]).astype(BF16)
        o0_scr[h, pl.ds(r0, R), :] = ox[:, :LANE]
        kt = (kn * jnp.exp(glcol - gcol)).astype(BF16)
        for c in range(R // CHUNK_B):
            gn = _tn(kt[c * CHUNK_B:(c + 1) * CHUNK_B], uwb[c * CHUNK_B:(c + 1) * CHUNK_B])
            n_scr[h, gi * (R // CHUNK_B) + c] = gn[:, :LANE]
            g_scr[h, gi * (R // CHUNK_B) + c] = gn[:, LANE:].astype(BF16)

    def group(i, _):
        gens = [group_head(i * GU + u, h) for u in range(GU) for h in range(H_B)]
        while gens:
            gens = [g for g in gens if next(g, True) is None]
        return 0

    lax.fori_loop(0, S // (R * GU), group, 0)

    s_scr[...] = jnp.zeros(s_scr.shape, F32)

    def scan(c, _):
        r0 = pl.multiple_of(c * CHUNK_B, CHUNK_B)
        glrow = jnp.exp(gl_scr[pl.ds(r0, 1), :])
        for h in range(H_B):
            st = s_scr[h]
            sb = st.astype(BF16)
            o = _dot(qeff_scr[h, pl.ds(r0, CHUNK_B), :], sb) + o0_scr[h, pl.ds(r0, CHUNK_B), :]
            o = _rmsnorm_rows(o, ng_ref[...])
            o_ref[pl.ds(r0, CHUNK_B), h * LANE:(h + 1) * LANE] = o.astype(BF16)
            a = glrow[:, MISC_DECAY + h:MISC_DECAY + h + 1]
            s_scr[h] = a * st - _dot(g_scr[h, c], sb) + n_scr[h, c]
        return 0

    lax.fori_loop(0, NCHUNK, scan, 0, unroll=2)


def mixer_b(pn, conv_w, alog_row, dt_row, out_norm):
    B = pn.shape[0] // SEQ
    S = SEQ
    W = 3 * H_B * DK_B
    row = pl.BlockSpec((1, LANE), lambda b: (0, 0))
    return pl.pallas_call(
        _mixb_kernel,
        grid=(B,),
        in_specs=[pl.BlockSpec((S, W), lambda b: (b, PN_B // W), pipeline_mode=pl.Buffered(1)),
                  pl.BlockSpec((S, LANE), lambda b: (b, PN_MISC // LANE)),
                  pl.BlockSpec((CONV_B, W), lambda b: (0, 0)), row, row, row],
        out_specs=pl.BlockSpec((S, BRANCH_W), lambda b: (b, 0)),
        out_shape=jax.ShapeDtypeStruct((B * S, BRANCH_W), BF16),
        scratch_shapes=[pltpu.VMEM((W // LANE, PAD_B + S, LANE), F32),
                        pltpu.VMEM((S, LANE), F32),
                        pltpu.VMEM((S, LANE), F32),
                        pltpu.VMEM((S // GC, 8, GC), F32),
                        pltpu.VMEM((H_B, S, LANE), BF16),
                        pltpu.VMEM((H_B, S, LANE), F32),
                        pltpu.VMEM((H_B, NCHUNK, DK_B, DV_B), F32),
                        pltpu.VMEM((H_B, NCHUNK, DK_B, DK_B), BF16),
                        pltpu.VMEM((H_B, DK_B, DV_B), F32)],
        compiler_params=_cparams(1),
        name="mixer_b",
    )(pn, pn, conv_w, alog_row, dt_row, out_norm)


TM_MERGE = 512


def _merge_kernel(*refs):
    o_refs = refs[0:4]
    z_refs = refs[4:8]
    m_refs = refs[8:12]
    x_ref, mod_ref, wbr_ref, wout_ref, lng_ref, lnb_ref, out_ref = refs[12:]
    merged = jnp.zeros((TM_MERGE, D_MODEL), F32)
    for n in range(N_BRANCH):
        z = z_refs[n][...]
        br = o_refs[n][...] * (z * _sigmoid_tanh(z))
        merged = merged + _dot(br, wbr_ref[n]) * _sigmoid_tanh(m_refs[n][...]).astype(F32)
    y = _dot(merged.astype(BF16), wout_ref[...])
    t = DEEPNORM_ALPHA * x_ref[...] + mod_ref[0, 2:3, :] * y
    mu = jnp.mean(t, axis=-1, keepdims=True)
    tc = t - mu
    var = jnp.mean(tc * tc, axis=-1, keepdims=True)
    out_ref[...] = tc * lax.rsqrt(var + NORM_EPS) * lng_ref[...] + lnb_ref[...]


def merge_out(pn, o_a, o_b, o_c, o_d, x2, mod, w_br, w_out, ln_g, ln_b):
    T = x2.shape[0]
    tm = TM_MERGE
    per_b = SEQ // tm
    br = pl.BlockSpec((tm, BRANCH_W), lambda i: (i, 0))
    zspec = lambda n: pl.BlockSpec((tm, BRANCH_W), lambda i: (i, PN_Z // BRANCH_W + n))
    mspec = lambda n: pl.BlockSpec((tm, D_MODEL), lambda i: (i, PN_MERGE // D_MODEL + n))
    row = pl.BlockSpec((1, D_MODEL), lambda i: (0, 0))
    return pl.pallas_call(
        _merge_kernel,
        grid=(T // tm,),
        in_specs=[br, br, br, br] + [zspec(n) for n in range(4)] + [mspec(n) for n in range(4)] + [
            pl.BlockSpec((tm, D_MODEL), lambda i: (i, 0)),
            pl.BlockSpec((1, 3, D_MODEL), lambda i: (i // per_b, 0, 0)),
            pl.BlockSpec((N_BRANCH, BRANCH_W, D_MODEL), lambda i: (0, 0, 0)),
            pl.BlockSpec((D_MODEL, D_MODEL), lambda i: (0, 0)), row, row],
        out_specs=pl.BlockSpec((tm, D_MODEL), lambda i: (i, 0)),
        out_shape=jax.ShapeDtypeStruct((T, D_MODEL), F32),
        compiler_params=_cparams(1),
        name="merge_out",
    )(o_a, o_b, o_c, o_d, pn, pn, pn, pn, pn, pn, pn, pn, x2, mod, w_br, w_out, ln_g, ln_b)


def _prep_w_in(w):
    zc = lambda n: jnp.zeros((D_MODEL, n), w.dtype)
    misc = jnp.concatenate([w[:, _OFF_BETA:_OFF_BETA + 4], w[:, _OFF_DECAY:_OFF_DECAY + 4], zc(MISC_KPE - 8),
                            w[:, _OFF_KPE:_OFF_KPE + ROPE_C]], axis=1)
    wn = jnp.concatenate([w[:, _OFF_A:_OFF_A + 1536], w[:, _OFF_BQKV:_OFF_BQKV + 1536],
                          w[:, _OFF_CQ:_OFF_CQ + Q_LORA_C], w[:, _OFF_CKV:_OFF_CKV + KV_LORA_C], misc,
                          w[:, _OFF_D:_OFF_Z], w[:, _OFF_Z:_OFF_Z + 2048], w[:, _OFF_MERGE:_OFF_MERGE + 4096]],
                         axis=1)
    return wn.astype(BF16)


def _prep_w_c(w_uq, w_ukv):
    q = w_uq.reshape(Q_LORA_C, H_C, NOPE_C + ROPE_C)
    wq = jnp.concatenate([q[..., :NOPE_C], jnp.zeros((Q_LORA_C, H_C, LANE - NOPE_C - ROPE_C), q.dtype),
                          q[..., NOPE_C:]], axis=-1).reshape(Q_LORA_C, H_C * LANE)
    half = ROPE_C // 2
    wqs = jnp.concatenate([jnp.zeros((Q_LORA_C, H_C, LANE - ROPE_C), q.dtype), q[..., NOPE_C + half:],
                           q[..., NOPE_C:NOPE_C + half]], axis=-1).reshape(Q_LORA_C, H_C * LANE)
    kv = w_ukv.reshape(KV_LORA_C, H_C, NOPE_C + V_C)
    wk = jnp.concatenate([kv[..., :NOPE_C], jnp.zeros((KV_LORA_C, H_C, LANE - NOPE_C), kv.dtype)],
                         axis=-1).reshape(KV_LORA_C, H_C * LANE)
    wvt = kv[..., NOPE_C:].reshape(KV_LORA_C, H_C * V_C).T
    return wq.astype(BF16), wqs.astype(BF16), wk.astype(BF16), wvt.astype(BF16)


def _lane_row(vals, at):
    return jnp.zeros((1, LANE), F32).at[0, at:at + vals.shape[0]].set(vals.astype(F32))


def kernel(x, c, positions, w_ada, b_ada, w_in, conv_b, a_log, dt_bias, out_norm_b, lambda_q1, lambda_k1,
           lambda_q2, lambda_k2, subln_g, q_norm_c, w_uq, kv_norm_c, w_ukv, w_br, w_out, ln_g, ln_b):
    B, S, D = x.shape
    assert (S, D) == (SEQ, D_MODEL) and w_in.shape[0] == DEPTH
    T = B * S
    mods = ada_modulation(c, w_ada, b_ada)
    cos_a, sin_a, cos_c, sin_c = rope_tables(positions)
    x2 = x.reshape(T, D)
    for l in range(DEPTH):
        mod = mods[l]
        pn = input_projection(x2, mod, _prep_w_in(w_in[l]))
        lam = jnp.stack([lambda_q1[l], lambda_k1[l], lambda_q2[l], lambda_k2[l]]).astype(F32)
        o_a = mixer_a(pn, cos_a, sin_a, lam, subln_g[l].reshape(LANE, 1),
                      lam_init=0.8 - 0.6 * math.exp(-0.3 * l))
        o_b = mixer_b(pn, conv_b[l], _lane_row(a_log[l], MISC_DECAY), _lane_row(dt_bias[l], MISC_DECAY),
                      out_norm_b[l].reshape(1, LANE))
        wq, wqs, wk, wvt = _prep_w_c(w_uq[l], w_ukv[l])
        o_c = mixer_c(pn, cos_c, sin_c, q_norm_c[l].reshape(1, Q_LORA_C), kv_norm_c[l].reshape(1, KV_LORA_C),
                      wq, wqs, wk, wvt)
        o_d = mixer_d(pn, cos_a, sin_a)
        x2 = merge_out(pn, o_a, o_b, o_c, o_d, x2, mod, w_br[l].astype(BF16), w_out[l].astype(BF16),
                       ln_g[l].reshape(1, D), ln_b[l].reshape(1, D))
    return x2.reshape(B, S, D)
```

```python
import functools
import math

import jax
import jax.numpy as jnp
from jax import lax
from jax.experimental import pallas as pl
from jax.experimental.pallas import tpu as pltpu

F32 = jnp.float32
BF16 = jnp.bfloat16

D_MODEL = 1024
SEQ = 2048
DEPTH = 2
ROPE_THETA = 500000.0
NORM_EPS = 1e-6
H_A, DH_A = 4, 64
H_B, DK_B, DV_B, CONV_B, CHUNK_B = 4, 128, 128, 4, 64
H_C, Q_LORA_C, KV_LORA_C, NOPE_C, ROPE_C, V_C = 8, 256, 128, 64, 32, 64
H_D, DH_D = 8, 64
DIL_GROUPS = ((128, 1), (512, 4), (2048, 16))
N_BRANCH, BRANCH_W = 4, 512
DEEPNORM_ALPHA = (2.0 * DEPTH) ** 0.25

LANE = 128
VMEM_LIMIT = 56 * 1024 * 1024

_OFF_A = 0
_OFF_BQKV = 1536
_OFF_BETA = 3072
_OFF_DECAY = 3076
_OFF_CQ = 3080
_OFF_CKV = 3336
_OFF_KPE = 3464
_OFF_D = 3496
_OFF_Z = 8104
_OFF_MERGE = 10152
D_IN = 14248

PN_A = 0
PN_B = 1536
PN_CQ = 3072
PN_CKV = 3328
PN_MISC = 3456
PN_D = 3584
PN_Z = 8192
PN_MERGE = 10240
PN_W = 14336
MISC_BETA, MISC_DECAY, MISC_KPE = 0, 4, 96


def _cparams(n_grid):
    return pltpu.CompilerParams(dimension_semantics=("arbitrary",) * n_grid,
                                vmem_limit_bytes=VMEM_LIMIT)


def _nt(a, b):
    return lax.dot_general(a, b, (((1,), (1,)), ((), ())), preferred_element_type=F32)


def _tn(a, b):
    return lax.dot_general(a, b, (((0,), (0,)), ((), ())), preferred_element_type=F32)


def _dot(a, b):
    return jnp.dot(a, b, preferred_element_type=F32)


def _sigmoid(x):
    return 1.0 / (1.0 + jnp.exp(-x))


def _sigmoid_tanh(x):
    return 0.5 * jnp.tanh(0.5 * x) + 0.5


def _silu(x):
    return x * _sigmoid(x)


def _ada_kernel(c_ref, w_ref, b_ref, o_ref):
    ca = _silu(c_ref[...]).astype(BF16)
    o_ref[0] = _dot(ca, w_ref[0].astype(BF16)) + b_ref[0]


def ada_modulation(c, w_ada, b_ada):
    B = c.shape[0]
    L = w_ada.shape[0]
    out = pl.pallas_call(
        _ada_kernel,
        grid=(L, 3),
        in_specs=[pl.BlockSpec((B, D_MODEL), lambda l, j: (0, 0)),
                  pl.BlockSpec((1, D_MODEL, D_MODEL), lambda l, j: (l, 0, j)),
                  pl.BlockSpec((1, 1, D_MODEL), lambda l, j: (l, 0, j))],
        out_specs=pl.BlockSpec((1, B, D_MODEL), lambda l, j: (l, 0, j)),
        out_shape=jax.ShapeDtypeStruct((L, B, 3 * D_MODEL), F32),
        compiler_params=_cparams(2),
        name="ada_modulation",
    )(c, w_ada, b_ada.reshape(L, 1, 3 * D_MODEL))
    return out.reshape(L, B, 3, D_MODEL)


def _rope_rows():
    half_a = (DH_A // 4) // 2
    inv_a = ROPE_THETA ** (-jnp.arange(half_a, dtype=F32) / half_a)
    half_c = ROPE_C // 2
    inv_c = ROPE_THETA ** (-jnp.arange(half_c, dtype=F32) / half_c)
    z = jnp.zeros
    inv = jnp.concatenate([inv_a, inv_a, z(48, F32), inv_a, inv_a, z(16, F32), inv_c, inv_c])
    sgn = jnp.concatenate([-jnp.ones(8, F32), jnp.ones(8, F32), z(48, F32),
                           -jnp.ones(8, F32), jnp.ones(8, F32), z(16, F32),
                           -jnp.ones(16, F32), jnp.ones(16, F32)])
    return inv.reshape(1, LANE), sgn.reshape(1, LANE)


def _rope_table_kernel(pos_ref, inv_ref, sgn_ref, cos_a, sin_a, cos_c, sin_c):
    R = 256

    def body(i, _):
        r0 = pl.multiple_of(i * R, R)
        ang = pos_ref[pl.ds(r0, R), :].astype(F32) * inv_ref[...]
        cv = jnp.cos(ang)
        sv = jnp.sin(ang) * sgn_ref[...]
        lane = lax.broadcasted_iota(jnp.int32, (R, LANE), 1)
        is_c = lane >= MISC_KPE
        cos_a[pl.ds(r0, R), :] = jnp.where(is_c, 1.0, cv)
        sin_a[pl.ds(r0, R), :] = jnp.where(is_c, 0.0, sv)
        cos_c[pl.ds(r0, R), :] = jnp.where(is_c, cv, 1.0)
        sin_c[pl.ds(r0, R), :] = jnp.where(is_c, sv, 0.0)
        return 0

    lax.fori_loop(0, SEQ // R, body, 0)


def rope_tables(positions):
    B, S = positions.shape
    inv, sgn = _rope_rows()
    tab = jax.ShapeDtypeStruct((B * S, LANE), F32)
    row = pl.BlockSpec((1, LANE), lambda b: (0, 0))
    blk = pl.BlockSpec((S, LANE), lambda b: (b, 0))
    return pl.pallas_call(
        _rope_table_kernel,
        grid=(B,),
        in_specs=[pl.BlockSpec((S, 1), lambda b: (b, 0)), row, row],
        out_specs=[blk, blk, blk, blk],
        out_shape=[tab, tab, tab, tab],
        compiler_params=_cparams(1),
        name="rope_tables",
    )(positions.reshape(B * S, 1), inv, sgn)


TM_IN = 1024
TN_IN = 2048


def _inproj_kernel(x_ref, mod_ref, w_ref, o_ref, h_ref):
    @pl.when(pl.program_id(1) == 0)
    def _():
        shift = mod_ref[0, 0:1, :]
        scale1 = 1.0 + mod_ref[0, 1:2, :]
        R = 128

        def body(i, _):
            r0 = pl.multiple_of(i * R, R)
            xs = x_ref[pl.ds(r0, R), :]
            mu = jnp.mean(xs, axis=-1, keepdims=True)
            xc = xs - mu
            var = jnp.mean(xc * xc, axis=-1, keepdims=True)
            hn = xc * lax.rsqrt(var + NORM_EPS)
            h_ref[pl.ds(r0, R), :] = (hn * scale1 + shift).astype(BF16)
            return 0

        lax.fori_loop(0, TM_IN // R, body, 0)

    o_ref[...] = _dot(h_ref[...], w_ref[...]).astype(BF16)


def input_projection(x2, mod, w):
    T = x2.shape[0]
    ncol = w.shape[1]
    return pl.pallas_call(
        _inproj_kernel,
        grid=(T // TM_IN, ncol // TN_IN),
        in_specs=[pl.BlockSpec((TM_IN, D_MODEL), lambda i, j: (i, 0)),
                  pl.BlockSpec((1, 3, D_MODEL), lambda i, j: (i // (SEQ // TM_IN), 0, 0)),
                  pl.BlockSpec((D_MODEL, TN_IN), lambda i, j: (0, j))],
        out_specs=pl.BlockSpec((TM_IN, TN_IN), lambda i, j: (i, j)),
        out_shape=jax.ShapeDtypeStruct((T, ncol), BF16),
        scratch_shapes=[pltpu.VMEM((TM_IN, D_MODEL), BF16)],
        compiler_params=_cparams(2),
        name="input_projection",
    )(x2, mod, w)


def _rope64(x, cos, sin, lane):
    sw = jnp.where((lane & 63) < 8, pltpu.roll(x, LANE - 8, 1), pltpu.roll(x, 8, 1))
    return x * cos + sw * sin


TQ = 256
TKB = 256
LOG2E = 1.4426950408889634


def _attention_pipeline(nq, nsub, operands, finish):
    items = [(qi, sub) for qi in range(nq) for sub in range(nsub)]
    state = {}

    def phase_a(it):
        qi, _ = it
        k_ref, q_blk, _, s_ref, _ = operands(*it)
        kmax = (qi + 1) * TQ
        m = None
        for c0 in range(0, kmax, 2 * TKB):
            c1 = min(c0 + 2 * TKB, kmax)
            s = _nt(k_ref[c0:c1, :], q_blk)
            if c1 == kmax:
                kk = lax.broadcasted_iota(jnp.int32, (c1 - c0, TQ), 0) + (c0 - qi * TQ)
                qq = lax.broadcasted_iota(jnp.int32, (c1 - c0, TQ), 1)
                s = jnp.where(kk <= qq, s, -jnp.inf)
            s_ref[c0:c1, :] = s
            mc = jnp.max(s, axis=0, keepdims=True)
            m = mc if m is None else jnp.maximum(m, mc)
            yield
        state[it] = {"m": m}

    def phase_b(it):
        qi, _ = it
        _, _, _, s_ref, p_ref = operands(*it)
        m = state[it]["m"]
        l = None
        for t in range(qi + 1):
            p = jnp.exp2(s_ref[t * TKB:(t + 1) * TKB, :] - m)
            p_ref[t * TKB:(t + 1) * TKB, :] = p.astype(BF16)
            lt = jnp.sum(p, axis=0, keepdims=True)
            l = lt if l is None else l + lt
            yield
        state[it]["l"] = l

    def phase_c(it):
        qi, _ = it
        _, _, vt_ref, _, p_ref = operands(*it)
        kmax = (qi + 1) * TQ
        acc = None
        for c0 in range(0, kmax, 2 * TKB):
            c1 = min(c0 + 2 * TKB, kmax)
            d = _dot(vt_ref[:, c0:c1], p_ref[c0:c1, :])
            acc = d if acc is None else acc + d
            yield
        state[it]["o"] = acc / state[it]["l"]

    n = len(items)
    for step in range(n + 2):
        gens = []
        if step < n:
            gens.append(phase_a(items[step]))
        if 0 <= step - 1 < n:
            gens.append(phase_b(items[step - 1]))
        if 0 <= step - 2 < n:
            gens.append(phase_c(items[step - 2]))
        while gens:
            gens = [g for g in gens if next(g, True) is None]
        done = step - 2
        if done >= 0 and items[done][1] == nsub - 1:
            qi = items[done][0]
            finish(qi, [state.pop((qi, sub))["o"] for sub in range(nsub)])


def _mixa_kernel(q_ref, k_ref, v_ref, cos_ref, sin_ref, lam_ref, g_ref, o_ref, q2_ref, kr_ref, vt_ref, s_ref, p_ref,
                 *, lam_init):
    S = SEQ
    R = 256

    def prep(i, _):
        r0 = pl.multiple_of(i * R, R)
        lane = lax.broadcasted_iota(jnp.int32, (R, LANE), 1)
        cos = cos_ref[pl.ds(r0, R), :]
        sin = sin_ref[pl.ds(r0, R), :]
        q = _rope64(q_ref[pl.ds(r0, R), :].astype(F32), cos, sin, lane) * (DH_A ** -0.5 * LOG2E)
        q2_ref[0, pl.ds(r0, R), :] = jnp.where(lane < 64, q, 0.0).astype(BF16)
        q2_ref[1, pl.ds(r0, R), :] = jnp.where(lane >= 64, q, 0.0).astype(BF16)
        kr_ref[pl.ds(r0, R), :] = _rope64(k_ref[pl.ds(r0, R), :].astype(F32), cos, sin, lane).astype(BF16)
        vt_ref[:, pl.ds(r0, R)] = v_ref[pl.ds(r0, R), :].astype(F32).T.astype(BF16)
        return 0

    lax.fori_loop(0, S // R, prep, 0)

    lp = lam_ref[...]
    lam = (jnp.exp(jnp.sum(lp[0:1] * lp[1:2], axis=-1, keepdims=True))
           - jnp.exp(jnp.sum(lp[2:3] * lp[3:4], axis=-1, keepdims=True)) + lam_init)
    gcol = g_ref[...] * (1.0 - lam_init)

    def finish(qi, outs):
        o = outs[0] - lam * outs[1]
        ms = jnp.mean(o * o, axis=0, keepdims=True)
        o = o * lax.rsqrt(ms + NORM_EPS) * gcol
        o_ref[qi * TQ:(qi + 1) * TQ, :] = o.T.astype(BF16)

    _attention_pipeline(S // TQ, 2,
                        lambda qi, mp: (kr_ref, q2_ref[mp, qi * TQ:(qi + 1) * TQ, :], vt_ref, s_ref.at[mp], p_ref.at[mp]),
                        finish)


def mixer_a(pn, cos_a, sin_a, lam_params, subln_g_col, *, lam_init=0.2):
    B = pn.shape[0] // SEQ
    S = SEQ
    col = lambda off: (lambda b, h: (b, off // LANE + h))
    blk = lambda off: pl.BlockSpec((S, LANE), col(off))
    tab = pl.BlockSpec((S, LANE), lambda b, h: (b, 0))
    return pl.pallas_call(
        functools.partial(_mixa_kernel, lam_init=lam_init),
        grid=(B, H_A),
        in_specs=[blk(PN_A), blk(PN_A + 512), blk(PN_A + 1024), tab, tab,
                  pl.BlockSpec((4, DH_A), lambda b, h: (0, 0)),
                  pl.BlockSpec((LANE, 1), lambda b, h: (0, 0))],
        out_specs=pl.BlockSpec((S, LANE), lambda b, h: (b, h)),
        out_shape=jax.ShapeDtypeStruct((B * S, BRANCH_W), BF16),
        scratch_shapes=[pltpu.VMEM((2, S, LANE), BF16), pltpu.VMEM((S, LANE), BF16), pltpu.VMEM((LANE, S), BF16),
                        pltpu.VMEM((2, S, TQ), F32), pltpu.VMEM((2, S, TQ), BF16)],
        compiler_params=_cparams(2),
        name="mixer_a",
    )(pn, pn, pn, cos_a, sin_a, lam_params, subln_g_col)


def _rmsnorm_rows(x, g):
    return x * lax.rsqrt(jnp.mean(x * x, axis=-1, keepdims=True) + NORM_EPS) * g


def _rope_c(x, cos, sin, lane):
    sw = jnp.where(lane < 112, pltpu.roll(x, LANE - 16, 1), pltpu.roll(x, 16, 1))
    return x * cos + sw * sin


def _mixc_kernel(cq_ref, ckv_ref, misc_ref, cos_ref, sin_ref, qg_ref, kvg_ref, wuq_ref, wuqs_ref, wuk_ref, wuvt_ref,
                 o_ref, q_scr, k_scr, vt_scr, s_ref, p_ref):
    S = SEQ
    R = 256
    scale = (NOPE_C + ROPE_C) ** -0.5 * LOG2E

    def prep(i, _):
        r0 = pl.multiple_of(i * R, R)
        lane = lax.broadcasted_iota(jnp.int32, (R, LANE), 1)
        cos = cos_ref[pl.ds(r0, R), :]
        sin = sin_ref[pl.ds(r0, R), :]
        cqn = _rmsnorm_rows(cq_ref[pl.ds(r0, R), :].astype(F32), qg_ref[...]).astype(BF16)
        q = _dot(cqn, wuq_ref[...])
        qsw = _dot(cqn, wuqs_ref[...])
        kvn = _rmsnorm_rows(ckv_ref[pl.ds(r0, R), :].astype(F32), kvg_ref[...]).astype(BF16)
        kk = _dot(kvn, wuk_ref[...])
        kpe = jnp.where(lane >= MISC_KPE, misc_ref[pl.ds(r0, R), :].astype(F32), 0.0)
        kpe = _rope_c(kpe, cos, sin, lane)
        for e in range(2):
            qe = (q[:, e * LANE:(e + 1) * LANE] * cos + qsw[:, e * LANE:(e + 1) * LANE] * sin) * scale
            q_scr[e, pl.ds(r0, R), :] = qe.astype(BF16)
            k_scr[e, pl.ds(r0, R), :] = (kk[:, e * LANE:(e + 1) * LANE] + kpe).astype(BF16)
        vt_scr[:, pl.ds(r0, R)] = _nt(wuvt_ref[...], kvn).astype(BF16)
        return 0

    lax.fori_loop(0, S // R, prep, 0)

    def finish(qi, outs):
        o_ref[qi * TQ:(qi + 1) * TQ, :] = jnp.concatenate(outs, axis=0).T.astype(BF16)

    _attention_pipeline(S // TQ, 2,
                        lambda qi, e: (k_scr.at[e], q_scr[e, qi * TQ:(qi + 1) * TQ, :],
                                       vt_scr.at[e * V_C:(e + 1) * V_C], s_ref.at[e], p_ref.at[e]),
                        finish)


def mixer_c(pn, cos_c, sin_c, q_norm, kv_norm, wuq, wuqs, wuk, wuvt):
    B = pn.shape[0] // SEQ
    S = SEQ
    tab = pl.BlockSpec((S, LANE), lambda b, p: (b, 0))
    return pl.pallas_call(
        _mixc_kernel,
        grid=(B, H_C // 2),
        in_specs=[pl.BlockSpec((S, Q_LORA_C), lambda b, p: (b, PN_CQ // Q_LORA_C)),
                  pl.BlockSpec((S, LANE), lambda b, p: (b, PN_CKV // LANE)),
                  pl.BlockSpec((S, LANE), lambda b, p: (b, PN_MISC // LANE)),
                  tab, tab,
                  pl.BlockSpec((1, Q_LORA_C), lambda b, p: (0, 0)),
                  pl.BlockSpec((1, KV_LORA_C), lambda b, p: (0, 0)),
                  pl.BlockSpec((Q_LORA_C, 2 * LANE), lambda b, p: (0, p)),
                  pl.BlockSpec((Q_LORA_C, 2 * LANE), lambda b, p: (0, p)),
                  pl.BlockSpec((KV_LORA_C, 2 * LANE), lambda b, p: (0, p)),
                  pl.BlockSpec((LANE, KV_LORA_C), lambda b, p: (p, 0))],
        out_specs=pl.BlockSpec((S, LANE), lambda b, p: (b, p)),
        out_shape=jax.ShapeDtypeStruct((B * S, BRANCH_W), BF16),
        scratch_shapes=[pltpu.VMEM((2, S, LANE), BF16), pltpu.VMEM((2, S, LANE), BF16), pltpu.VMEM((LANE, S), BF16),
                        pltpu.VMEM((2, S, TQ), F32), pltpu.VMEM((2, S, TQ), BF16)],
        compiler_params=_cparams(2),
        name="mixer_c",
    )(pn, pn, pn, cos_c, sin_c, q_norm, kv_norm, wuq, wuqs, wuk, wuvt)


DB = 128
LSE_W = 16


def _mixd_kernel(q0_ref, q1_ref, q2_ref, k0_ref, k1_ref, k2_ref, v0_ref, v1_ref, v2_ref, cos_ref, sin_ref, o_ref,
                 nat, qs, ks, vts, onat, lnat):
    S = SEQ
    NB = S // DB
    UB = 4
    UA = 16
    RB = UB * DB
    lane = lax.broadcasted_iota(jnp.int32, (RB, LANE), 1)
    lane_b = lax.broadcasted_iota(jnp.int32, (DB, LANE), 1)
    q_refs = (q0_ref, q1_ref, q2_ref)
    k_refs = (k0_ref, k1_ref, k2_ref)
    v_refs = (v0_ref, v1_ref, v2_ref)

    def nat_start(bi, d):
        nblk = NB // d
        return (bi % nblk) * (DB * d) + bi // nblk

    for g in range(len(DIL_GROUPS)):
        ks[g, 0:DB, :] = jnp.zeros((DB, LANE), BF16)
        vts[g, :, 0:DB] = jnp.zeros((LANE, DB), BF16)

    for g, (_, d) in enumerate(DIL_GROUPS):
        def sources(r0, g=g):
            cos = cos_ref[pl.ds(r0, RB), :]
            sin = sin_ref[pl.ds(r0, RB), :]
            q = _rope64(q_refs[g][pl.ds(r0, RB), :].astype(F32), cos, sin, lane) * (DH_D ** -0.5 * LOG2E)
            k = _rope64(k_refs[g][pl.ds(r0, RB), :].astype(F32), cos, sin, lane)
            return q, k, v_refs[g][pl.ds(r0, RB), :].astype(F32)

        def sink(bi, q, k, v, g=g):
            r0 = pl.multiple_of(bi * DB, DB)
            qs[g, 0, pl.ds(r0, DB), :] = jnp.where(lane_b < 64, q, 0.0).astype(BF16)
            qs[g, 1, pl.ds(r0, DB), :] = jnp.where(lane_b >= 64, q, 0.0).astype(BF16)
            ks[g, pl.ds(pl.multiple_of(DB + r0, DB), DB), :] = k.astype(BF16)
            vts[g, :, pl.ds(pl.multiple_of(DB + r0, DB), DB)] = v.T.astype(BF16)

        if d == 1:
            def direct(i, _):
                q, k, v = sources(pl.multiple_of(i * RB, RB))
                for t in range(UB):
                    rows = slice(t * DB, (t + 1) * DB)
                    sink(i * UB + t, q[rows], k[rows], v[rows])
                return 0
            lax.fori_loop(0, NB // UB, direct, 0)
        else:
            def stage(i, _):
                r0 = pl.multiple_of(i * RB, RB)
                for t, x in enumerate(sources(r0)):
                    nat[t, pl.ds(r0, RB), :] = x
                return 0
            lax.fori_loop(0, NB // UB, stage, 0)

            def gather(i, _, d=d):
                for t in range(UB):
                    bi = i * UB + t
                    rows = pl.ds(nat_start(bi, d), DB, stride=d)
                    sink(bi, nat[0, rows, :], nat[1, rows, :], nat[2, rows, :])
                return 0
            lax.fori_loop(0, NB // UB, gather, 0)

    def head(g, h, bi, nblk):
        r0 = pl.multiple_of(bi * DB, DB)
        if nblk == 1:
            nk, k0 = DB, pl.multiple_of(r0 + DB, DB)
            kk = lax.broadcasted_iota(jnp.int32, (DB, DB), 0)
            valid = kk <= lax.broadcasted_iota(jnp.int32, (DB, DB), 1)
        else:
            nk, k0 = 2 * DB, r0
            kk = lax.broadcasted_iota(jnp.int32, (2 * DB, DB), 0)
            qq = lax.broadcasted_iota(jnp.int32, (2 * DB, DB), 1) + DB
            lo = jnp.where(bi % nblk == 0, DB, 0)
            valid = (kk <= qq) & (kk >= jnp.maximum(qq - DB, lo))
        s = _nt(ks[g, pl.ds(k0, nk), :], qs[g, h, pl.ds(r0, DB), :])
        yield
        s = jnp.where(valid, s, -jnp.inf)
        m = jnp.max(s, axis=0, keepdims=True)
        p = jnp.exp2(s - m)
        l = jnp.sum(p, axis=0, keepdims=True)
        ot = _dot(vts[g, h * DH_D:(h + 1) * DH_D, pl.ds(k0, nk)], p.astype(BF16))
        yield
        yield ot / l, jnp.broadcast_to(m * (1.0 / LOG2E) + jnp.log(l), (DH_D, DB))

    for g, (_, d) in enumerate(DIL_GROUPS):
        def blocks(i, _, g=g, d=d):
            bis = [i * UA + t for t in range(UA)]
            gens = [head(g, h, bi, NB // d) for bi in bis for h in range(2)]
            for _ in range(2):
                for gen in gens:
                    next(gen)
            res = [next(gen) for gen in gens]
            for t, bi in enumerate(bis):
                (o0, l0), (o1, l1) = res[2 * t], res[2 * t + 1]
                start = nat_start(bi, d)
                rows = pl.ds(pl.multiple_of(start, DB), DB) if d == 1 else pl.ds(start, DB, stride=d)
                onat[g, rows, :] = jnp.concatenate([o0, o1], axis=0).T
                lnat[g, rows, :] = jnp.concatenate([l0, l1], axis=0).T
            return 0

        lax.fori_loop(0, NB // UA, blocks, 0)

    def merge(i, _):
        r0 = pl.multiple_of(i * DB, DB)
        ls = [lnat[g, pl.ds(r0, DB), :] for g in range(3)]
        mx = jnp.maximum(ls[0], jnp.maximum(ls[1], ls[2]))
        ws = [jnp.exp(l - mx) for l in ls]
        num = ws[0] * onat[0, pl.ds(r0, DB), :] + ws[1] * onat[1, pl.ds(r0, DB), :] + ws[2] * onat[2, pl.ds(r0, DB), :]
        o_ref[pl.ds(r0, DB), :] = (num / (ws[0] + ws[1] + ws[2])).astype(BF16)
        return 0

    lax.fori_loop(0, NB, merge, 0)


def mixer_d(pn, cos_a, sin_a):
    B = pn.shape[0] // SEQ
    S = SEQ
    ng = len(DIL_GROUPS)
    blk = lambda part, g: pl.BlockSpec((S, LANE), lambda b, p: (b, (PN_D + (part * ng + g) * BRANCH_W) // LANE + p))
    tab = pl.BlockSpec((S, LANE), lambda b, p: (b, 0))
    return pl.pallas_call(
        _mixd_kernel,
        grid=(B, H_D // 2),
        in_specs=[blk(part, g) for part in range(3) for g in range(ng)] + [tab, tab],
        out_specs=pl.BlockSpec((S, LANE), lambda b, p: (b, p)),
        out_shape=jax.ShapeDtypeStruct((B * S, BRANCH_W), BF16),
        scratch_shapes=[pltpu.VMEM((3, S, LANE), F32),
                        pltpu.VMEM((ng, 2, S, LANE), BF16),
                        pltpu.VMEM((ng, DB + S, LANE), BF16),
                        pltpu.VMEM((ng, LANE, DB + S), BF16),
                        pltpu.VMEM((ng, S, LANE), F32),
                        pltpu.VMEM((ng, S, LANE), F32)],
        compiler_params=_cparams(2),
        name="mixer_d",
    )(*([pn] * (3 * ng)), cos_a, sin_a)


GB = 128
GU = 4
GC = 256
NCHUNK = SEQ // CHUNK_B
PAD_B = 8


def _split3(x):
    hi = x.astype(BF16)
    r1 = x - hi.astype(F32)
    mid = r1.astype(BF16)
    lo = (r1 - mid.astype(F32)).astype(BF16)
    return hi, mid, lo


def _mixb_kernel(qkv_ref, misc_ref, conv_ref, alog_ref, dt_ref, ng_ref, o_ref,
                 xpad, gc_scr, gl_scr, gct_scr, qeff_scr, o0_scr, n_scr, g_scr, s_scr):
    S = SEQ
    RC = GC
    R = GB
    ri = lax.broadcasted_iota(jnp.int32, (R, R), 0)
    ci = lax.broadcasted_iota(jnp.int32, (R, R), 1)
    same = (ri // CHUNK_B) == (ci // CHUNK_B)
    tril = same & (ri >= ci)
    strict = same & (ri > ci)
    eye = jnp.where(ri == ci, 1.0, 0.0)

    rc = lax.broadcasted_iota(jnp.int32, (RC, RC), 0)
    cc = lax.broadcasted_iota(jnp.int32, (RC, RC), 1)
    same_c = (rc // CHUNK_B) == (cc // CHUNK_B)
    sel = jnp.concatenate([jnp.where(same_c & (rc >= cc), 1.0, 0.0), jnp.where(same_c, 1.0, 0.0)],
                          axis=0).astype(BF16)
    neg_a = -jnp.exp(alog_ref[...])
    xpad[:, 0:PAD_B, :] = jnp.zeros((xpad.shape[0], PAD_B, LANE), F32)

    def gprep(i, _):
        r0 = pl.multiple_of(i * RC, RC)
        for cb in range(xpad.shape[0]):
            xpad[cb, pl.ds(PAD_B + r0, RC), :] = qkv_ref[pl.ds(r0, RC), cb * LANE:(cb + 1) * LANE].astype(F32)
        x = misc_ref[pl.ds(r0, RC), :].astype(F32) + dt_ref[...]
        sp = jnp.maximum(x, 0.0) + jnp.log1p(jnp.exp(-jnp.abs(x)))
        g = neg_a * sp
        hi, mid, lo = _split3(g)
        acc = _dot(sel, jnp.concatenate([hi, mid, lo], axis=1))
        acc = acc[:, 0:LANE] + acc[:, LANE:2 * LANE] + acc[:, 2 * LANE:3 * LANE]
        gc_scr[pl.ds(r0, RC), :] = acc[:RC]
        gl_scr[pl.ds(r0, RC), :] = acc[RC:]
        gct_scr[i] = acc[:RC].T[0:8, :]
        return 0

    lax.fori_loop(0, S // RC, gprep, 0)

    def conv_cols(r0, cb, norm):
        cs = slice(cb * LANE, (cb + 1) * LANE)
        y = jnp.zeros((R, LANE), F32)
        for j in range(CONV_B):
            y = y + xpad[cb, pl.ds(r0 + (PAD_B - (CONV_B - 1) + j), R), :] * conv_ref[j:j + 1, cs]
        y = _silu(y)
        if norm:
            y = y * lax.rsqrt(jnp.sum(y * y, axis=-1, keepdims=True) + NORM_EPS)
        return y

    def group_head(gi, h):
        r0 = pl.multiple_of(gi * R, R)
        qn = conv_cols(r0, h, True)
        kn = conv_cols(r0, H_B + h, True)
        v = conv_cols(r0, 2 * H_B + h, False)
        mf = misc_ref[pl.ds(r0, R), :].astype(F32)
        beta = _sigmoid(mf[:, MISC_BETA + h:MISC_BETA + h + 1])
        gcol = gc_scr[pl.ds(r0, R), MISC_DECAY + h:MISC_DECAY + h + 1]
        glcol = gl_scr[pl.ds(r0, R), MISC_DECAY + h:MISC_DECAY + h + 1]
        grow = gct_scr[gi // (RC // R), MISC_DECAY + h:MISC_DECAY + h + 1,
                       pl.ds(pl.multiple_of((gi % (RC // R)) * R, R), R)]
        dec = jnp.exp(jnp.where(tril, gcol - grow, 0.0))
        kb = kn * beta
        knb = kn.astype(BF16)
        lm = jnp.where(strict, _nt(kb.astype(BF16), knb) * dec, 0.0)
        qkm = jnp.where(tril, _nt((qn * (DK_B ** -0.5)).astype(BF16), knb) * dec, 0.0)
        yield
        x = eye - jnp.where((ri // 2) == (ci // 2), lm, 0.0)
        s = 2
        while s < CHUNK_B:
            off = ((ri // (2 * s)) == (ci // (2 * s))) & ((ri // s) != (ci // s))
            xb = x.astype(BF16)
            t = _dot(jnp.where(off, lm, 0.0).astype(BF16), xb).astype(BF16)
            yield
            x = x - _dot(xb, t)
            yield
            s *= 2
        rhs = jnp.concatenate([v * beta, kb * jnp.exp(gcol)], axis=1).astype(BF16)
        uw = _dot(x.astype(BF16), rhs)
        yield
        uwb = uw.astype(BF16)
        ox = _dot(qkm.astype(BF16), uwb)
        yield
        qeff_scr[h, pl.ds(r0, R), :] = (qn * (DK_B ** -0.5) * jnp.exp(gcol) - ox[:, LANE:]).astype(BF16)
        o0_scr[h, pl.ds(r0, R), :] = ox[:, :LANE]
        kt = (kn * jnp.exp(glcol - gcol)).astype(BF16)
        for c in range(R // CHUNK_B):
            gn = _tn(kt[c * CHUNK_B:(c + 1) * CHUNK_B], uwb[c * CHUNK_B:(c + 1) * CHUNK_B])
            n_scr[h, gi * (R // CHUNK_B) + c] = gn[:, :LANE]
            g_scr[h, gi * (R // CHUNK_B) + c] = gn[:, LANE:].astype(BF16)

    def group(i, _):
        gens = [group_head(i * GU + u, h) for u in range(GU) for h in range(H_B)]
        while gens:
            gens = [g for g in gens if next(g, True) is None]
        return 0

    lax.fori_loop(0, S // (R * GU), group, 0)

    s_scr[...] = jnp.zeros(s_scr.shape, F32)

    def scan(c, _):
        r0 = pl.multiple_of(c * CHUNK_B, CHUNK_B)
        glrow = jnp.exp(gl_scr[pl.ds(r0, 1), :])
        for h in range(H_B):
            st = s_scr[h]
            sb = st.astype(BF16)
            o = _dot(qeff_scr[h, pl.ds(r0, CHUNK_B), :], sb) + o0_scr[h, pl.ds(r0, CHUNK_B), :]
            o = _rmsnorm_rows(o, ng_ref[...])
            o_ref[pl.ds(r0, CHUNK_B), h * LANE:(h + 1) * LANE] = o.astype(BF16)
            a = glrow[:, MISC_DECAY + h:MISC_DECAY + h + 1]
            s_scr[h] = a * st - _dot(g_scr[h, c], sb) + n_scr[h, c]
        return 0

    lax.fori_loop(0, NCHUNK, scan, 0, unroll=2)


def mixer_b(pn, conv_w, alog_row, dt_row, out_norm):
    B = pn.shape[0] // SEQ
    S = SEQ
    W = 3 * H_B * DK_B
    row = pl.BlockSpec((1, LANE), lambda b: (0, 0))
    return pl.pallas_call(
        _mixb_kernel,
        grid=(B,),
        in_specs=[pl.BlockSpec((S, W), lambda b: (b, PN_B // W), pipeline_mode=pl.Buffered(1)),
                  pl.BlockSpec((S, LANE), lambda b: (b, PN_MISC // LANE)),
                  pl.BlockSpec((CONV_B, W), lambda b: (0, 0)), row, row, row],
        out_specs=pl.BlockSpec((S, BRANCH_W), lambda b: (b, 0)),
        out_shape=jax.ShapeDtypeStruct((B * S, BRANCH_W), BF16),
        scratch_shapes=[pltpu.VMEM((W // LANE, PAD_B + S, LANE), F32),
                        pltpu.VMEM((S, LANE), F32),
                        pltpu.VMEM((S, LANE), F32),
                        pltpu.VMEM((S // GC, 8, GC), F32),
                        pltpu.VMEM((H_B, S, LANE), BF16),
                        pltpu.VMEM((H_B, S, LANE), F32),
                        pltpu.VMEM((H_B, NCHUNK, DK_B, DV_B), F32),
                        pltpu.VMEM((H_B, NCHUNK, DK_B, DK_B), BF16),
                        pltpu.VMEM((H_B, DK_B, DV_B), F32)],
        compiler_params=_cparams(1),
        name="mixer_b",
    )(pn, pn, conv_w, alog_row, dt_row, out_norm)


TM_MERGE = 512


def _merge_kernel(*refs):
    o_refs = refs[0:4]
    z_refs = refs[4:8]
    m_refs = refs[8:12]
    x_ref, mod_ref, wbr_ref, wout_ref, lng_ref, lnb_ref, out_ref = refs[12:]
    merged = jnp.zeros((TM_MERGE, D_MODEL), F32)
    for n in range(N_BRANCH):
        z = z_refs[n][...]
        br = o_refs[n][...] * (z * _sigmoid_tanh(z))
        merged = merged + _dot(br, wbr_ref[n]) * _sigmoid_tanh(m_refs[n][...]).astype(F32)
    y = _dot(merged.astype(BF16), wout_ref[...])
    t = DEEPNORM_ALPHA * x_ref[...] + mod_ref[0, 2:3, :] * y
    mu = jnp.mean(t, axis=-1, keepdims=True)
    tc = t - mu
    var = jnp.mean(tc * tc, axis=-1, keepdims=True)
    out_ref[...] = tc * lax.rsqrt(var + NORM_EPS) * lng_ref[...] + lnb_ref[...]


def merge_out(pn, o_a, o_b, o_c, o_d, x2, mod, w_br, w_out, ln_g, ln_b):
    T = x2.shape[0]
    tm = TM_MERGE
    per_b = SEQ // tm
    br = pl.BlockSpec((tm, BRANCH_W), lambda i: (i, 0))
    zspec = lambda n: pl.BlockSpec((tm, BRANCH_W), lambda i: (i, PN_Z // BRANCH_W + n))
    mspec = lambda n: pl.BlockSpec((tm, D_MODEL), lambda i: (i, PN_MERGE // D_MODEL + n))
    row = pl.BlockSpec((1, D_MODEL), lambda i: (0, 0))
    return pl.pallas_call(
        _merge_kernel,
        grid=(T // tm,),
        in_specs=[br, br, br, br] + [zspec(n) for n in range(4)] + [mspec(n) for n in range(4)] + [
            pl.BlockSpec((tm, D_MODEL), lambda i: (i, 0)),
            pl.BlockSpec((1, 3, D_MODEL), lambda i: (i // per_b, 0, 0)),
            pl.BlockSpec((N_BRANCH, BRANCH_W, D_MODEL), lambda i: (0, 0, 0)),
            pl.BlockSpec((D_MODEL, D_MODEL), lambda i: (0, 0)), row, row],
        out_specs=pl.BlockSpec((tm, D_MODEL), lambda i: (i, 0)),
        out_shape=jax.ShapeDtypeStruct((T, D_MODEL), F32),
        compiler_params=_cparams(1),
        name="merge_out",
    )(o_a, o_b, o_c, o_d, pn, pn, pn, pn, pn, pn, pn, pn, x2, mod, w_br, w_out, ln_g, ln_b)


def _prep_w_in(w):
    zc = lambda n: jnp.zeros((D_MODEL, n), w.dtype)
    misc = jnp.concatenate([w[:, _OFF_BETA:_OFF_BETA + 4], w[:, _OFF_DECAY:_OFF_DECAY + 4], zc(MISC_KPE - 8),
                            w[:, _OFF_KPE:_OFF_KPE + ROPE_C]], axis=1)
    wn = jnp.concatenate([w[:, _OFF_A:_OFF_A + 1536], w[:, _OFF_BQKV:_OFF_BQKV + 1536],
                          w[:, _OFF_CQ:_OFF_CQ + Q_LORA_C], w[:, _OFF_CKV:_OFF_CKV + KV_LORA_C], misc,
                          w[:, _OFF_D:_OFF_Z], w[:, _OFF_Z:_OFF_Z + 2048], w[:, _OFF_MERGE:_OFF_MERGE + 4096]],
                         axis=1)
    return wn.astype(BF16)


def _prep_w_c(w_uq, w_ukv):
    q = w_uq.reshape(Q_LORA_C, H_C, NOPE_C + ROPE_C)
    wq = jnp.concatenate([q[..., :NOPE_C], jnp.zeros((Q_LORA_C, H_C, LANE - NOPE_C - ROPE_C), q.dtype),
                          q[..., NOPE_C:]], axis=-1).reshape(Q_LORA_C, H_C * LANE)
    half = ROPE_C // 2
    wqs = jnp.concatenate([jnp.zeros((Q_LORA_C, H_C, LANE - ROPE_C), q.dtype), q[..., NOPE_C + half:],
                           q[..., NOPE_C:NOPE_C + half]], axis=-1).reshape(Q_LORA_C, H_C * LANE)
    kv = w_ukv.reshape(KV_LORA_C, H_C, NOPE_C + V_C)
    wk = jnp.concatenate([kv[..., :NOPE_C], jnp.zeros((KV_LORA_C, H_C, LANE - NOPE_C), kv.dtype)],
                         axis=-1).reshape(KV_LORA_C, H_C * LANE)
    wvt = kv[..., NOPE_C:].reshape(KV_LORA_C, H_C * V_C).T
    return wq.astype(BF16), wqs.astype(BF16), wk.astype(BF16), wvt.astype(BF16)


def _lane_row(vals, at):
    return jnp.zeros((1, LANE), F32).at[0, at:at + vals.shape[0]].set(vals.astype(F32))


def kernel(x, c, positions, w_ada, b_ada, w_in, conv_b, a_log, dt_bias, out_norm_b, lambda_q1, lambda_k1,
           lambda_q2, lambda_k2, subln_g, q_norm_c, w_uq, kv_norm_c, w_ukv, w_br, w_out, ln_g, ln_b):
    B, S, D = x.shape
    assert (S, D) == (SEQ, D_MODEL) and w_in.shape[0] == DEPTH
    T = B * S
    mods = ada_modulation(c, w_ada, b_ada)
    cos_a, sin_a, cos_c, sin_c = rope_tables(positions)
    x2 = x.reshape(T, D)
    for l in range(DEPTH):
        mod = mods[l]
        pn = input_projection(x2, mod, _prep_w_in(w_in[l]))
        lam = jnp.stack([lambda_q1[l], lambda_k1[l], lambda_q2[l], lambda_k2[l]]).astype(F32)
        o_a = mixer_a(pn, cos_a, sin_a, lam, subln_g[l].reshape(LANE, 1),
                      lam_init=0.8 - 0.6 * math.exp(-0.3 * l))
        o_b = mixer_b(pn, conv_b[l], _lane_row(a_log[l], MISC_DECAY), _lane_row(dt_bias[l], MISC_DECAY),
                      out_norm_b[l].reshape(1, LANE))
        wq, wqs, wk, wvt = _prep_w_c(w_uq[l], w_ukv[l])
        o_c = mixer_c(pn, cos_c, sin_c, q_norm_c[l].reshape(1, Q_LORA_C), kv_norm_c[l].reshape(1, KV_LORA_C),
                      wq, wqs, wk, wvt)
        o_d = mixer_d(pn, cos_a, sin_a)
        x2 = merge_out(pn, o_a, o_b, o_c, o_d, x2, mod, w_br[l].astype(BF16), w_out[l].astype(BF16),
                       ln_g[l].reshape(1, D), ln_b[l].reshape(1, D))
    return x2.reshape(B, S, D)
```
